```python
import math
import jax, jax.numpy as jnp
from jax import lax
import numpy as np

D_MODEL = 2048
BATCH = 4
SEQ = 8192
DEPTH = 4

N_MIXERS = 3
N_CONV_LAYERS = (DEPTH + 2) // 3
N_ATTN_LAYERS = (DEPTH + 1) // 3
N_GDN_LAYERS = DEPTH // 3

N_META = 16
BLOCK = 128
FRONT = BLOCK
PAD_LEN = FRONT - N_META

NORM_EPS = 1e-6
LN_EPS = 1e-5
NEG_INF = -1e30

CONV_KERNEL = 31

HEAD_DIM = 64
N_HEADS = D_MODEL // HEAD_DIM
N_KV_HEADS = N_HEADS // 8
GROUP = N_HEADS // N_KV_HEADS
WINDOW = 128
ROPE_THETA = 10000.0
Q_WIDTH = N_HEADS * HEAD_DIM
KV_WIDTH = N_KV_HEADS * HEAD_DIM
QKV_WIDTH = Q_WIDTH + 2 * KV_WIDTH

GDN_K_DIM = 128
GDN_V_DIM = 128
GDN_K_HEADS = D_MODEL // 128
GDN_V_HEADS = 2 * GDN_K_HEADS
GDN_KEY_WIDTH = GDN_K_HEADS * GDN_K_DIM
GDN_VAL_WIDTH = GDN_V_HEADS * GDN_V_DIM
GDN_CONV_DIM = 2 * GDN_KEY_WIDTH + GDN_VAL_WIDTH
GDN_IN_WIDTH = GDN_CONV_DIM + GDN_VAL_WIDTH + 2 * GDN_V_HEADS
GDN_CONV = 4
GDN_CHUNK = 64

FFN_HIDDEN = ((8 * D_MODEL // 3 + 255) // 256) * 256

kernel_name = "hybrid_conformer_swa_sink_gdn_trunk"


def rms_norm(x, w):
    x32 = x.astype(jnp.float32)
    y = x32 * lax.rsqrt(jnp.mean(x32 * x32, axis=-1, keepdims=True) + NORM_EPS)
    return (y * w.astype(jnp.float32)).astype(x.dtype)


def layer_norm(x, g, b):
    x32 = x.astype(jnp.float32)
    mu = jnp.mean(x32, axis=-1, keepdims=True)
    xc = x32 - mu
    var = jnp.mean(xc * xc, axis=-1, keepdims=True)
    y = xc * lax.rsqrt(var + LN_EPS) * g.astype(jnp.float32) + b.astype(jnp.float32)
    return y.astype(x.dtype)


def l2_normalize(x):
    return x * lax.rsqrt(jnp.sum(x * x, axis=-1, keepdims=True) + 1e-6)


def causal_depthwise_conv(u, w):
    k = w.shape[0]
    return lax.conv_general_dilated(
        u, w[:, None, :].astype(u.dtype), window_strides=(1,), padding=[(k - 1, 0)],
        dimension_numbers=("NWC", "WIO", "NWC"), feature_group_count=u.shape[-1])


def swiglu_ffn(h, w_gate, w_up, w_down):
    return (jax.nn.silu(h @ w_gate) * (h @ w_up)) @ w_down


def conformer_conv(h, valid, w_pw1, b_pw1, w_dw, b_dw, ln_g, ln_b, w_pw2, b_pw2):
    u = h @ w_pw1 + b_pw1
    val, gate = jnp.split(u, 2, axis=-1)
    u = val * jax.nn.sigmoid(gate)
    u = jnp.where(valid[None, :, None], u, 0)
    u = causal_depthwise_conv(u, w_dw) + b_dw
    u = jax.nn.silu(layer_norm(u, ln_g, ln_b))
    return u @ w_pw2 + b_pw2


def apply_rope(x, cos, sin):
    x1, x2 = jnp.split(x, 2, axis=-1)
    c = cos[None, :, None, :].astype(x.dtype)
    s = sin[None, :, None, :].astype(x.dtype)
    return jnp.concatenate([x1 * c - x2 * s, x2 * c + x1 * s], axis=-1)


def sliding_window_sink_attention(h, cos, sin, w_qkv, b_qkv, sinks, w_o, b_o):
    bsz, length, _ = h.shape
    n_blocks = length // BLOCK
    qkv = h @ w_qkv + b_qkv
    q = qkv[..., :Q_WIDTH].reshape(bsz, length, N_HEADS, HEAD_DIM)
    k = qkv[..., Q_WIDTH:Q_WIDTH + KV_WIDTH].reshape(bsz, length, N_KV_HEADS, HEAD_DIM)
    v = qkv[..., Q_WIDTH + KV_WIDTH:].reshape(bsz, length, N_KV_HEADS, HEAD_DIM)
    q = apply_rope(q, cos, sin)
    k = apply_rope(k, cos, sin)
    q = q.reshape(bsz, n_blocks, BLOCK, N_KV_HEADS, GROUP, HEAD_DIM)
    k = k.reshape(bsz, n_blocks, BLOCK, N_KV_HEADS, HEAD_DIM)
    v = v.reshape(bsz, n_blocks, BLOCK, N_KV_HEADS, HEAD_DIM)

    def band(t):
        prev = jnp.pad(t, ((0, 0), (1, 0), (0, 0), (0, 0), (0, 0)))[:, :-1]
        return jnp.concatenate([prev, t], axis=2)

    kb, vb = band(k), band(v)
    scores = jnp.einsum("bnqkgd,bnskd->bnkgqs", q, kb).astype(jnp.float32) * (HEAD_DIM ** -0.5)
    blk = jnp.arange(n_blocks)
    q_idx = blk[:, None] * BLOCK + jnp.arange(BLOCK)[None, :]
    k_idx = (blk[:, None] - 1) * BLOCK + jnp.arange(2 * BLOCK)[None, :]
    dist = q_idx[:, :, None] - k_idx[:, None, :]
    allowed = (dist >= 0) & (dist < WINDOW) & (k_idx[:, None, :] >= PAD_LEN)
    scores = jnp.where(allowed[None, :, None, None], scores, NEG_INF)
    sink = sinks.astype(jnp.float32).reshape(N_KV_HEADS, GROUP)[None, None, :, :, None, None]
    m = jnp.maximum(jnp.max(scores, axis=-1, keepdims=True), sink)
    e = jnp.exp(scores - m)
    probs = e / (jnp.sum(e, axis=-1, keepdims=True) + jnp.exp(sink - m))
    out = jnp.einsum("bnkgqs,bnskd->bnqkgd", probs.astype(vb.dtype), vb)
    out = out.reshape(bsz, length, Q_WIDTH)
    return out @ w_o + b_o


def chunk_gated_delta_rule(q, k, v, g, beta):
    bsz, length, heads, dk = q.shape
    dv = v.shape[-1]
    n = length // GDN_CHUNK
    q = l2_normalize(q) * (dk ** -0.5)
    k = l2_normalize(k)

    def to_chunks(t):
        return t.reshape(bsz, n, GDN_CHUNK, heads, t.shape[-1]).transpose(1, 0, 3, 2, 4)

    q, k, v = to_chunks(q), to_chunks(k), to_chunks(v)
    g = g.reshape(bsz, n, GDN_CHUNK, heads).transpose(1, 0, 3, 2)
    beta = beta.reshape(bsz, n, GDN_CHUNK, heads).transpose(1, 0, 3, 2)
    gc = jnp.cumsum(g, axis=-1)
    idx = jnp.arange(GDN_CHUNK)
    causal = idx[:, None] >= idx[None, :]
    strict = idx[:, None] > idx[None, :]
    decay = jnp.exp(jnp.where(causal, gc[..., :, None] - gc[..., None, :], -jnp.inf))
    k_beta = k * beta[..., None]
    lower = jnp.where(strict, jnp.einsum("nbhid,nbhjd->nbhij", k_beta, k) * decay, 0.0)
    tri = jnp.eye(GDN_CHUNK, dtype=jnp.float32) + lower
    rhs = jnp.concatenate([v * beta[..., None], k_beta * jnp.exp(gc)[..., None]], axis=-1)
    sol = lax.linalg.triangular_solve(tri, rhs, left_side=True, lower=True)
    u, w = sol[..., :dv], sol[..., dv:]
    intra = jnp.einsum("nbhid,nbhjd->nbhij", q, k) * decay
    q_dec = q * jnp.exp(gc)[..., None]
    k_dec = k * jnp.exp(gc[..., -1:] - gc)[..., None]
    chunk_decay = jnp.exp(gc[..., -1])

    def step(state, xs):
        q_c, k_c, u_c, w_c, a_c, cd = xs
        v_new = u_c - jnp.einsum("bhck,bhkv->bhcv", w_c, state)
        out = (jnp.einsum("bhck,bhkv->bhcv", q_c, state)
               + jnp.einsum("bhcs,bhsv->bhcv", a_c, v_new))
        state = state * cd[..., None, None] + jnp.einsum("bhck,bhcv->bhkv", k_c, v_new)
        return state, out

    state0 = jnp.zeros((bsz, heads, dk, dv), jnp.float32)
    _, out = lax.scan(step, state0, (q_dec, k_dec, u, w, intra, chunk_decay))
    return out.transpose(1, 0, 3, 2, 4).reshape(bsz, length, heads, dv)


def gated_deltanet(h, valid, w_in, conv_w, a_log, dt_bias, norm_w, w_out):
    bsz, length, _ = h.shape
    proj = h @ w_in
    qkv = proj[..., :GDN_CONV_DIM]
    z = proj[..., GDN_CONV_DIM:GDN_CONV_DIM + GDN_VAL_WIDTH]
    b = proj[..., GDN_CONV_DIM + GDN_VAL_WIDTH:GDN_CONV_DIM + GDN_VAL_WIDTH + GDN_V_HEADS]
    a = proj[..., GDN_CONV_DIM + GDN_VAL_WIDTH + GDN_V_HEADS:]
    mask = valid[None, :, None]
    qkv = jnp.where(mask, qkv, 0)
    qkv = jax.nn.silu(causal_depthwise_conv(qkv, conv_w)).astype(jnp.float32)
    q = qkv[..., :GDN_KEY_WIDTH].reshape(bsz, length, GDN_K_HEADS, GDN_K_DIM)
    k = qkv[..., GDN_KEY_WIDTH:2 * GDN_KEY_WIDTH].reshape(bsz, length, GDN_K_HEADS, GDN_K_DIM)
    v = qkv[..., 2 * GDN_KEY_WIDTH:].reshape(bsz, length, GDN_V_HEADS, GDN_V_DIM)
    rep = GDN_V_HEADS // GDN_K_HEADS
    q = jnp.repeat(q, rep, axis=2)
    k = jnp.repeat(k, rep, axis=2)
    beta = jnp.where(mask, jax.nn.sigmoid(b.astype(jnp.float32)), 0.0)
    g = jnp.where(mask, -jnp.exp(a_log.astype(jnp.float32))
                  * jax.nn.softplus(a.astype(jnp.float32) + dt_bias.astype(jnp.float32)), 0.0)
    o = chunk_gated_delta_rule(q, k, v, g, beta)
    zg = jax.nn.silu(z.astype(jnp.float32).reshape(bsz, length, GDN_V_HEADS, GDN_V_DIM))
    o = rms_norm(o, norm_w) * zg
    return o.reshape(bsz, length, GDN_VAL_WIDTH).astype(h.dtype) @ w_out


def setup_inputs(seed: int = 0) -> dict:
    key = jax.random.key(seed)
    ks = iter(jax.random.split(key, 40))
    f32 = jnp.float32
    D = D_MODEL

    def nrm(shape, scale):
        return jax.random.normal(next(ks), shape, f32) * scale

    def gain(shape):
        return 1.0 + nrm(shape, 0.02)

    dt = jnp.exp(jax.random.uniform(next(ks), (N_GDN_LAYERS, GDN_V_HEADS), f32,
                                    math.log(1e-3), math.log(1e-1)))
    return {
        "x": nrm((BATCH, SEQ, D), 1.0),
        "meta_tokens": nrm((N_META, D), 1.0),
        "norm_mix": gain((DEPTH, D)),
        "norm_ffn": gain((DEPTH, D)),
        "norm_final": gain((D,)),
        "conv_w_pw1": nrm((N_CONV_LAYERS, D, 2 * D), D ** -0.5),
        "conv_b_pw1": nrm((N_CONV_LAYERS, 2 * D), 0.01),
        "conv_w_dw": nrm((N_CONV_LAYERS, CONV_KERNEL, D), CONV_KERNEL ** -0.5),
        "conv_b_dw": nrm((N_CONV_LAYERS, D), 0.01),
        "conv_ln_g": gain((N_CONV_LAYERS, D)),
        "conv_ln_b": nrm((N_CONV_LAYERS, D), 0.01),
        "conv_w_pw2": nrm((N_CONV_LAYERS, D, D), D ** -0.5),
        "conv_b_pw2": nrm((N_CONV_LAYERS, D), 0.01),
        "attn_w_qkv": nrm((N_ATTN_LAYERS, D, QKV_WIDTH), D ** -0.5),
        "attn_b_qkv": nrm((N_ATTN_LAYERS, QKV_WIDTH), 0.01),
        "attn_sinks": nrm((N_ATTN_LAYERS, N_HEADS), 1.0),
        "attn_w_o": nrm((N_ATTN_LAYERS, Q_WIDTH, D), Q_WIDTH ** -0.5),
        "attn_b_o": nrm((N_ATTN_LAYERS, D), 0.01),
        "gdn_w_in": nrm((N_GDN_LAYERS, D, GDN_IN_WIDTH), D ** -0.5),
        "gdn_conv_w": nrm((N_GDN_LAYERS, GDN_CONV, GDN_CONV_DIM), GDN_CONV ** -0.5),
        "gdn_a_log": jnp.log(jax.random.uniform(next(ks), (N_GDN_LAYERS, GDN_V_HEADS), f32, 1.0, 16.0)),
        "gdn_dt_bias": dt + jnp.log(-jnp.expm1(-dt)),
        "gdn_norm_w": gain((N_GDN_LAYERS, GDN_V_DIM)),
        "gdn_w_out": nrm((N_GDN_LAYERS, GDN_VAL_WIDTH, D), GDN_VAL_WIDTH ** -0.5),
        "ffn_w_gate": nrm((DEPTH, D, FFN_HIDDEN), D ** -0.5),
        "ffn_w_up": nrm((DEPTH, D, FFN_HIDDEN), D ** -0.5),
        "ffn_w_down": nrm((DEPTH, FFN_HIDDEN, D), FFN_HIDDEN ** -0.5),
    }


def reference(x, meta_tokens, norm_mix, norm_ffn, norm_final,
              conv_w_pw1, conv_b_pw1, conv_w_dw, conv_b_dw, conv_ln_g, conv_ln_b,
              conv_w_pw2, conv_b_pw2,
              attn_w_qkv, attn_b_qkv, attn_sinks, attn_w_o, attn_b_o,
              gdn_w_in, gdn_conv_w, gdn_a_log, gdn_dt_bias, gdn_norm_w, gdn_w_out,
              ffn_w_gate, ffn_w_up, ffn_w_down):
    bsz = x.shape[0]
    meta = jnp.broadcast_to(meta_tokens.astype(x.dtype)[None], (bsz, N_META, D_MODEL))
    h = jnp.concatenate([jnp.zeros((bsz, PAD_LEN, D_MODEL), x.dtype), meta, x], axis=1)
    length = h.shape[1]
    valid = jnp.arange(length) >= PAD_LEN
    pos = (jnp.arange(length) - PAD_LEN).astype(jnp.float32)
    inv_freq = ROPE_THETA ** (-jnp.arange(0, HEAD_DIM, 2, dtype=jnp.float32) / HEAD_DIM)
    ang = pos[:, None] * inv_freq[None, :]
    cos, sin = jnp.cos(ang), jnp.sin(ang)

    for i in range(DEPTH):
        kind, j = i % N_MIXERS, i // N_MIXERS
        hn = rms_norm(h, norm_mix[i])
        if kind == 0:
            mix = conformer_conv(hn, valid, conv_w_pw1[j], conv_b_pw1[j], conv_w_dw[j], conv_b_dw[j],
                                 conv_ln_g[j], conv_ln_b[j], conv_w_pw2[j], conv_b_pw2[j])
        elif kind == 1:
            mix = sliding_window_sink_attention(hn, cos, sin, attn_w_qkv[j], attn_b_qkv[j],
                                                attn_sinks[j], attn_w_o[j], attn_b_o[j])
        else:
            mix = gated_deltanet(hn, valid, gdn_w_in[j], gdn_conv_w[j], gdn_a_log[j],
                                 gdn_dt_bias[j], gdn_norm_w[j], gdn_w_out[j])
        h = h + mix.astype(h.dtype)
        h = h + swiglu_ffn(rms_norm(h, norm_ffn[i]), ffn_w_gate[i], ffn_w_up[i], ffn_w_down[i])

    return rms_norm(h, norm_final)[:, FRONT:]
```

```python
import functools
import math

import jax
import jax.numpy as jnp
from jax import lax
from jax.experimental import pallas as pl
from jax.experimental.pallas import tpu as pltpu

F32 = jnp.float32
BF16 = jnp.bfloat16

N_META = 16
FRONT = 128
PAD_LEN = FRONT - N_META
NORM_EPS = 1e-6
LN_EPS = 1e-5
NEG_INF = -1e30
CONV_KERNEL = 31
HEAD_DIM = 64
ATTN_BLOCK = 128
ROPE_THETA = 10000.0
GDN_DIM = 128
GDN_CONV = 4
GDN_CHUNK = 64

LANES = 128
SUBLANES = 8
VMEM_LIMIT_BYTES = 56 * 1024 * 1024

CONV_HALO = 32
GDN_ROWS = 2 * GDN_CHUNK


def _params(*sem):
    return pltpu.CompilerParams(dimension_semantics=sem, vmem_limit_bytes=VMEM_LIMIT_BYTES)


def _tile(n, target, mult):
    best = None
    for d in range(mult, min(n, target) + 1, mult):
        if n % d == 0:
            best = d
    assert best is not None, (n, target, mult)
    return best


def _rms(x, w):
    ms = jnp.mean(x * x, axis=-1, keepdims=True)
    return x * lax.rsqrt(ms + NORM_EPS) * w


def _sigmoid(x):
    return 1.0 / (1.0 + jnp.exp(-x))


def _silu(x):
    return x * _sigmoid(x)


def _dot(a, b):
    return jnp.dot(a, b, preferred_element_type=F32)


def _dot_nt(a, b):
    return lax.dot_general(a, b, (((1,), (1,)), ((), ())), preferred_element_type=F32)


def _dot_tn(a, b):
    return lax.dot_general(a, b, (((0,), (0,)), ((), ())), preferred_element_type=F32)


def _split3(x):
    hi = x.astype(BF16)
    r1 = x - hi.astype(F32)
    mid = r1.astype(BF16)
    lo = (r1 - mid.astype(F32)).astype(BF16)
    return hi, mid, lo


def _ffn_kernel(x_ref, nw_ref, wg_ref, wu_ref, wd_ref, o_ref, hn_ref, acc_ref):
    f = pl.program_id(1)

    @pl.when(f == 0)
    def _():
        hn_ref[...] = _rms(x_ref[...], nw_ref[...]).astype(BF16)
        acc_ref[...] = jnp.zeros_like(acc_ref)

    hn = hn_ref[...]
    g = _dot(hn, wg_ref[...])
    u = _dot(hn, wu_ref[...])
    a = (_silu(g) * u).astype(BF16)
    acc_ref[...] += _dot(a, wd_ref[...])

    @pl.when(f == pl.num_programs(1) - 1)
    def _():
        o_ref[...] = x_ref[...] + acc_ref[...]


def _ffn(h, nw, wg, wu, wd):
    t, d = h.shape
    f = wg.shape[1]
    tm = _tile(t, 640, LANES)
    tf = _tile(f, 512, LANES)
    return pl.pallas_call(
        _ffn_kernel,
        grid=(t // tm, f // tf),
        in_specs=[
            pl.BlockSpec((tm, d), lambda i, j: (i, 0)),
            pl.BlockSpec((1, d), lambda i, j: (0, 0)),
            pl.BlockSpec((d, tf), lambda i, j: (0, j)),
            pl.BlockSpec((d, tf), lambda i, j: (0, j)),
            pl.BlockSpec((tf, d), lambda i, j: (j, 0)),
        ],
        out_specs=pl.BlockSpec((tm, d), lambda i, j: (i, 0)),
        out_shape=jax.ShapeDtypeStruct((t, d), F32),
        scratch_shapes=[pltpu.VMEM((tm, d), BF16), pltpu.VMEM((tm, d), F32)],
        compiler_params=_params("parallel", "arbitrary"),
        name="ffn",
    )(h, nw.reshape(1, d), wg, wu, wd)


def _proj_res_kernel(a_ref, w_ref, b_ref, r_ref, o_ref):
    o_ref[...] = r_ref[...] + _dot(a_ref[...], w_ref[...]) + b_ref[...]


def _proj_res(a, w, b, res):
    t, k = a.shape
    n = w.shape[1]
    tm = _tile(t, 640, LANES)
    tn = _tile(n, 512, LANES)
    return pl.pallas_call(
        _proj_res_kernel,
        grid=(t // tm, n // tn),
        in_specs=[
            pl.BlockSpec((tm, k), lambda i, j: (i, 0)),
            pl.BlockSpec((k, tn), lambda i, j: (0, j)),
            pl.BlockSpec((1, tn), lambda i, j: (0, j)),
            pl.BlockSpec((tm, tn), lambda i, j: (i, j)),
        ],
        out_specs=pl.BlockSpec((tm, tn), lambda i, j: (i, j)),
        out_shape=jax.ShapeDtypeStruct((t, n), F32),
        compiler_params=_params("parallel", "arbitrary"),
        name="proj_res",
    )(a, w, b.reshape(1, n), res)


def _pw1_glu_kernel(x_ref, nw_ref, wv_ref, wg_ref, bv_ref, bg_ref, o_ref, hn_ref):
    @pl.when(pl.program_id(1) == 0)
    def _():
        hn_ref[...] = _rms(x_ref[...], nw_ref[...]).astype(BF16)

    hn = hn_ref[...]
    val = _dot(hn, wv_ref[...]) + bv_ref[...]
    gate = _dot(hn, wg_ref[...]) + bg_ref[...]
    o_ref[...] = val * _sigmoid(gate)


def _pw1_glu(h, nw, w, b):
    t, d = h.shape
    tm = _tile(t, 640, LANES)
    tn = _tile(d, 512, LANES)
    nj = d // tn
    b2 = b.reshape(1, 2 * d)
    return pl.pallas_call(
        _pw1_glu_kernel,
        grid=(t // tm, nj),
        in_specs=[
            pl.BlockSpec((tm, d), lambda i, j: (i, 0)),
            pl.BlockSpec((1, d), lambda i, j: (0, 0)),
            pl.BlockSpec((d, tn), lambda i, j: (0, j)),
            pl.BlockSpec((d, tn), lambda i, j: (0, j + nj)),
            pl.BlockSpec((1, tn), lambda i, j: (0, j)),
            pl.BlockSpec((1, tn), lambda i, j: (0, j + nj)),
        ],
        out_specs=pl.BlockSpec((tm, tn), lambda i, j: (i, j)),
        out_shape=jax.ShapeDtypeStruct((t, d), F32),
        scratch_shapes=[pltpu.VMEM((tm, d), BF16)],
        compiler_params=_params("parallel", "arbitrary"),
        name="conv_pw1_glu",
    )(h, nw.reshape(1, d), w, w, b2, b2)


def _dwconv_pw2_kernel(uc_ref, up_ref, wdw_ref, bdw_ref, lng_ref, lnb_ref, w2_ref, b2_ref,
                       r_ref, o_ref, xx_ref, y_ref, a_ref, *, tm, d):
    li = pl.program_id(1)
    rb = 64
    win = rb + CONV_HALO + SUBLANES

    @pl.when(pl.program_id(2) == 0)
    def _():
        pos = li * tm + lax.broadcasted_iota(jnp.int32, (tm, 1), 0)
        ppos = li * tm - CONV_HALO + lax.broadcasted_iota(jnp.int32, (CONV_HALO, 1), 0)
        xx_ref[0:CONV_HALO, :] = jnp.where(ppos >= PAD_LEN, up_ref[0], 0.0)
        xx_ref[CONV_HALO:CONV_HALO + tm, :] = jnp.where(pos >= PAD_LEN, uc_ref[0], 0.0)
        xx_ref[CONV_HALO + tm:, :] = jnp.zeros((SUBLANES, d), F32)

        base = CONV_HALO - CONV_KERNEL + 1
        for cb in range(d // LANES):
            cs = slice(cb * LANES, (cb + 1) * LANES)

            def row_block(r, carry, cs=cs):
                r0 = pl.multiple_of(r * rb, rb)
                w_in = xx_ref[pl.ds(r0, win), cs]
                acc = jnp.zeros((rb, LANES), F32)
                for s in range(SUBLANES):
                    ws = w_in if s == 0 else pltpu.roll(w_in, win - s, axis=0)
                    for k in range(CONV_KERNEL):
                        off = base + k
                        if off % SUBLANES == s:
                            a0 = off - s
                            acc = acc + wdw_ref[k:k + 1, cs] * ws[a0:a0 + rb]
                y_ref[pl.ds(r0, rb), cs] = acc + bdw_ref[:, cs]
                return carry

            lax.fori_loop(0, tm // rb, row_block, 0)

        y = y_ref[...]
        mu = jnp.mean(y, axis=-1, keepdims=True)
        yc = y - mu
        var = jnp.mean(yc * yc, axis=-1, keepdims=True)
        ln = yc * lax.rsqrt(var + LN_EPS) * lng_ref[...] + lnb_ref[...]
        a_ref[...] = _silu(ln).astype(BF16)

    o_ref[0] = r_ref[0] + _dot(a_ref[...], w2_ref[...]) + b2_ref[...]


def _dwconv_pw2(u, h, wdw, bdw, lng, lnb, w2, b2):
    bsz, length, d = u.shape
    tm = _tile(length, 640, 64)
    assert tm % CONV_HALO == 0
    tn = _tile(d, 512, LANES)
    halo_blocks = tm // CONV_HALO
    wdw_p = jnp.zeros((CONV_HALO, d), F32).at[:CONV_KERNEL].set(wdw)
    row = lambda v: v.reshape(1, d)
    kern = functools.partial(_dwconv_pw2_kernel, tm=tm, d=d)
    return pl.pallas_call(
        kern,
        grid=(bsz, length // tm, d // tn),
        in_specs=[
            pl.BlockSpec((1, tm, d), lambda b, l, j: (b, l, 0)),
            pl.BlockSpec((1, CONV_HALO, d), lambda b, l, j: (b, jnp.maximum(l * halo_blocks - 1, 0), 0)),
            pl.BlockSpec((CONV_HALO, d), lambda b, l, j: (0, 0)),
            pl.BlockSpec((1, d), lambda b, l, j: (0, 0)),
            pl.BlockSpec((1, d), lambda b, l, j: (0, 0)),
            pl.BlockSpec((1, d), lambda b, l, j: (0, 0)),
            pl.BlockSpec((d, tn), lambda b, l, j: (0, j)),
            pl.BlockSpec((1, tn), lambda b, l, j: (0, j)),
            pl.BlockSpec((1, tm, tn), lambda b, l, j: (b, l, j)),
        ],
        out_specs=pl.BlockSpec((1, tm, tn), lambda b, l, j: (b, l, j)),
        out_shape=jax.ShapeDtypeStruct((bsz, length, d), F32),
        scratch_shapes=[
            pltpu.VMEM((CONV_HALO + tm + SUBLANES, d), F32),
            pltpu.VMEM((tm, d), F32),
            pltpu.VMEM((tm, d), BF16),
        ],
        compiler_params=_params("parallel", "parallel", "arbitrary"),
        name="conv_dw_pw2",
    )(u, u, wdw_p, row(bdw), row(lng), row(lnb), w2, b2.reshape(1, d), h)


def _qkv_rope_kernel(x_ref, nw_ref, w_ref, b_ref, cos_ref, sin_ref, o_ref, hn_ref, *, n_q, n_rope):
    j = pl.program_id(1)

    @pl.when(j == 0)
    def _():
        hn_ref[...] = _rms(x_ref[...], nw_ref[...]).astype(BF16)

    acc = _dot(hn_ref[...], w_ref[...]) + b_ref[...]
    tn = acc.shape[1]

    @pl.when(j < n_rope)
    def _():
        lane = lax.broadcasted_iota(jnp.int32, acc.shape, 1)
        first = (lane & (HEAD_DIM - 1)) < (HEAD_DIM // 2)
        rot = jnp.where(first, pltpu.roll(acc, tn - HEAD_DIM // 2, axis=1),
                        pltpu.roll(acc, HEAD_DIM // 2, axis=1))
        r = acc * cos_ref[...] + rot * sin_ref[...]
        scale = jnp.where(j < n_q, HEAD_DIM ** -0.5, 1.0)
        o_ref[...] = (r * scale).astype(BF16)

    @pl.when(j >= n_rope)
    def _():
        o_ref[...] = acc.astype(BF16)


def _qkv_rope(h, nw, w, b, cos_t, sin_t, length, q_width, kv_width):
    t, d = h.shape
    n = w.shape[1]
    tn = kv_width
    assert tn % LANES == 0 and q_width % tn == 0 and n == q_width + 2 * kv_width
    tm = _tile(length, 640, LANES)
    nl = length // tm
    kern = functools.partial(_qkv_rope_kernel, n_q=q_width // tn, n_rope=q_width // tn + 1)
    return pl.pallas_call(
        kern,
        grid=(t // tm, n // tn),
        in_specs=[
            pl.BlockSpec((tm, d), lambda i, j: (i, 0)),
            pl.BlockSpec((1, d), lambda i, j: (0, 0)),
            pl.BlockSpec((d, tn), lambda i, j: (0, j)),
            pl.BlockSpec((1, tn), lambda i, j: (0, j)),
            pl.BlockSpec((tm, tn), lambda i, j: (i % nl, 0)),
            pl.BlockSpec((tm, tn), lambda i, j: (i % nl, 0)),
        ],
        out_specs=pl.BlockSpec((tm, tn), lambda i, j: (i, j)),
        out_shape=jax.ShapeDtypeStruct((t, n), BF16),
        scratch_shapes=[pltpu.VMEM((tm, d), BF16)],
        compiler_params=_params("parallel", "arbitrary"),
        name="attn_qkv_rope",
    )(h, nw.reshape(1, d), w, b.reshape(1, n), cos_t, sin_t)


def _attn_kernel(sink_ref, q_ref, kc_ref, kp_ref, vc_ref, vp_ref, o_ref, *, n_heads, n_kv):
    n = pl.program_id(1)
    blk = ATTN_BLOCK
    group = n_heads // n_kv
    qi = lax.broadcasted_iota(jnp.int32, (blk, 2 * blk), 0)
    ks = lax.broadcasted_iota(jnp.int32, (blk, 2 * blk), 1)
    k_idx = (n - 1) * blk + ks
    allowed = (ks > qi) & (ks <= qi + blk) & (k_idx >= PAD_LEN)
    for kv in range(n_kv):
        hs = slice(kv * HEAD_DIM, (kv + 1) * HEAD_DIM)
        kb = jnp.concatenate([kp_ref[0, :, hs], kc_ref[0, :, hs]], axis=0)
        vb = jnp.concatenate([vp_ref[0, :, hs], vc_ref[0, :, hs]], axis=0)
        for g in range(group):
            head = kv * group + g
            qs = slice(head * HEAD_DIM, (head + 1) * HEAD_DIM)
            sc = jnp.where(allowed, _dot_nt(q_ref[0, :, qs], kb), NEG_INF)
            sink = sink_ref[head]
            m = jnp.maximum(jnp.max(sc, axis=-1, keepdims=True), sink)
            e = jnp.exp(sc - m)
            denom = jnp.sum(e, axis=-1, keepdims=True) + jnp.exp(sink - m)
            p = (e / denom).astype(BF16)
            o_ref[0, :, qs] = _dot(p, vb).astype(BF16)


def _attention(qkv, sinks, n_heads, n_kv):
    bsz, length, _ = qkv.shape
    blk = ATTN_BLOCK
    qw = n_heads * HEAD_DIM
    kvw = n_kv * HEAD_DIM
    kcol = qw // kvw
    kern = functools.partial(_attn_kernel, n_heads=n_heads, n_kv=n_kv)
    prev = lambda n: jnp.maximum(n - 1, 0)
    return pl.pallas_call(
        kern,
        grid=(bsz, length // blk),
        in_specs=[
            pl.BlockSpec(memory_space=pltpu.SMEM),
            pl.BlockSpec((1, blk, qw), lambda b, n: (b, n, 0)),
            pl.BlockSpec((1, blk, kvw), lambda b, n: (b, n, kcol)),
            pl.BlockSpec((1, blk, kvw), lambda b, n: (b, prev(n), kcol)),
            pl.BlockSpec((1, blk, kvw), lambda b, n: (b, n, kcol + 1)),
            pl.BlockSpec((1, blk, kvw), lambda b, n: (b, prev(n), kcol + 1)),
        ],
        out_specs=pl.BlockSpec((1, blk, qw), lambda b, n: (b, n, 0)),
        out_shape=jax.ShapeDtypeStruct((bsz, length, qw), BF16),
        compiler_params=_params("parallel", "arbitrary"),
        name="attn_core",
    )(sinks, qkv, qkv, qkv, qkv, qkv)


def _gdn_in_kernel(x_ref, nw_ref, w_ref, wg_ref, o_ref, g_ref, hn_ref):
    @pl.when(pl.program_id(1) == 0)
    def _():
        hn_ref[...] = _rms(x_ref[...], nw_ref[...]).astype(BF16)
        g_ref[...] = _dot_nt(wg_ref[...], hn_ref[...])

    acc = _dot(hn_ref[...], w_ref[...])
    for c in range(o_ref.shape[0]):
        o_ref[c] = acc[:, c * LANES:(c + 1) * LANES].astype(BF16)


def _gdn_in(h, nw, w_main, w_gate_t):
    t, d = h.shape
    n = w_main.shape[1]
    tm = _tile(t, 640, LANES)
    tn = _tile(n, 512, LANES)
    cpb = tn // LANES
    return pl.pallas_call(
        _gdn_in_kernel,
        grid=(t // tm, n // tn),
        in_specs=[
            pl.BlockSpec((tm, d), lambda i, j: (i, 0)),
            pl.BlockSpec((1, d), lambda i, j: (0, 0)),
            pl.BlockSpec((d, tn), lambda i, j: (0, j)),
            pl.BlockSpec((LANES, d), lambda i, j: (0, 0)),
        ],
        out_specs=[
            pl.BlockSpec((cpb, tm, LANES), lambda i, j: (j, i, 0)),
            pl.BlockSpec((LANES, tm), lambda i, j: (0, i)),
        ],
        out_shape=[
            jax.ShapeDtypeStruct((n // LANES, t, LANES), BF16),
            jax.ShapeDtypeStruct((LANES, t), F32),
        ],
        scratch_shapes=[pltpu.VMEM((tm, d), BF16)],
        compiler_params=_params("parallel", "arbitrary"),
        name="gdn_in_proj",
    )(h, nw.reshape(1, d), w_main, w_gate_t)


def _gdn_kernel(alog_ref, dtb_ref, q_ref, k_ref, v_ref, z_ref, gate_ref, cwq_ref, cwk_ref, cwv_ref,
                nw_ref, o_ref, s_ref, tail_ref):
    kh = pl.program_id(1)
    ti = pl.program_id(2)
    rows = GDN_ROWS
    ch = GDN_CHUNK
    dk = GDN_DIM

    @pl.when(ti == 0)
    def _():
        s_ref[...] = jnp.zeros_like(s_ref)
        tail_ref[...] = jnp.zeros_like(tail_ref)

    pos_col = ti * rows + lax.broadcasted_iota(jnp.int32, (rows, 1), 0)
    valid_col = pos_col >= PAD_LEN

    def conv_silu(x_raw, slot, w_ref, wi):
        x = jnp.where(valid_col, x_raw.astype(F32), 0.0)
        xx = jnp.concatenate([tail_ref[slot], x], axis=0)
        tail_ref[slot] = x[rows - SUBLANES:, :]
        y = w_ref[GDN_CONV - 1, wi] * x
        for k in range(GDN_CONV - 1):
            shifted = pltpu.roll(xx, GDN_CONV - 1 - k, axis=0)
            y = y + w_ref[k, wi] * shifted[SUBLANES:, :]
        return _silu(y)

    def l2n(x):
        return x * lax.rsqrt(jnp.sum(x * x, axis=-1, keepdims=True) + 1e-6)

    q = l2n(conv_silu(q_ref[0, 0], 0, cwq_ref, 0)) * (dk ** -0.5)
    k = l2n(conv_silu(k_ref[0, 0], 1, cwk_ref, 0))
    vs = [conv_silu(v_ref[0, 0], 2, cwv_ref, 0), conv_silu(v_ref[1, 0], 3, cwv_ref, 1)]

    b_log = gate_ref[0, 0, 0]
    a_log = gate_ref[0, 1, 0]
    hsel = lax.broadcasted_iota(jnp.int32, (2, rows), 0)
    pos_row = ti * rows + lax.broadcasted_iota(jnp.int32, (2, rows), 1)
    valid_row = pos_row >= PAD_LEN
    neg_a = -jnp.exp(jnp.where(hsel == 0, alog_ref[2 * kh], alog_ref[2 * kh + 1]))
    dt = jnp.where(hsel == 0, dtb_ref[2 * kh], dtb_ref[2 * kh + 1])
    xs = a_log + dt
    softplus = jnp.maximum(xs, 0.0) + jnp.log1p(jnp.exp(-jnp.abs(xs)))
    g_row = jnp.where(valid_row, neg_a * softplus, 0.0)
    beta_row = jnp.where(valid_row, _sigmoid(b_log), 0.0)

    zpad = jnp.zeros((LANES - 2, rows), F32)
    g_pad = jnp.concatenate([g_row, zpad], axis=0)
    beta_pad = jnp.concatenate([beta_row, zpad], axis=0)
    ri = lax.broadcasted_iota(jnp.int32, (rows, rows), 0)
    ci = lax.broadcasted_iota(jnp.int32, (rows, rows), 1)
    shift = int(math.log2(ch))
    same_chunk = jnp.right_shift(ri, shift) == jnp.right_shift(ci, shift)
    upper = jnp.where(same_chunk & (ri <= ci), 1.0, 0.0).astype(BF16)
    lower = jnp.where(same_chunk & (ci <= ri), 1.0, 0.0).astype(BF16)
    eye = jnp.where(ri == ci, 1.0, 0.0).astype(BF16)
    g3 = _split3(g_pad)
    b3 = _split3(beta_pad)
    gc_row = _dot(g3[0], upper) + _dot(g3[1], upper) + _dot(g3[2], upper)
    gc_col = _dot_nt(lower, g3[0]) + _dot_nt(lower, g3[1]) + _dot_nt(lower, g3[2])
    beta_col = _dot_nt(eye, b3[0]) + _dot_nt(eye, b3[1]) + _dot_nt(eye, b3[2])

    ii = lax.broadcasted_iota(jnp.int32, (ch, ch), 0)
    jj = lax.broadcasted_iota(jnp.int32, (ch, ch), 1)
    causal = ii >= jj
    strict = ii > jj
    ident = jnp.where(ii == jj, 1.0, 0.0)

    for c in range(rows // ch):
        r0, r1 = c * ch, (c + 1) * ch
        qc = q[r0:r1]
        kc = k[r0:r1]
        qc_b = qc.astype(BF16)
        kc_b = kc.astype(BF16)
        kk = _dot_nt(kc_b, kc_b)
        qk = _dot_nt(qc_b, kc_b)
        for hv in range(2):
            gcol = gc_col[r0:r1, hv:hv + 1]
            grow = gc_row[hv:hv + 1, r0:r1]
            glast = gc_col[r1 - 1:r1, hv:hv + 1]
            bcol = beta_col[r0:r1, hv:hv + 1]
            decay = jnp.where(causal, jnp.exp(gcol - grow), 0.0)
            lm = jnp.where(strict, kk * bcol * decay, 0.0)
            x = ident - lm
            lm_b = lm.astype(BF16)
            p = _dot(lm_b, lm_b)
            for r in range(5):
                p_b = p.astype(BF16)
                x = x + _dot(x.astype(BF16), p_b)
                if r < 4:
                    p = _dot(p_b, p_b)
            egc = jnp.exp(gcol)
            kbeta = kc * bcol
            rhs = jnp.concatenate([vs[hv][r0:r1] * bcol, kbeta * egc], axis=1).astype(BF16)
            sol = _dot(x.astype(BF16), rhs)
            u = sol[:, :dk]
            w = sol[:, dk:]
            intra = jnp.where(causal, qk * decay, 0.0).astype(BF16)
            q_dec = qc * egc
            k_dec = (kc * jnp.exp(glast - gcol)).astype(BF16)
            cd = jnp.exp(glast)
            state = s_ref[hv]
            wq = jnp.concatenate([w, q_dec], axis=0).astype(BF16)
            ws_qs = _dot(wq, state.astype(BF16))
            v_new = u - ws_qs[:ch]
            v_new_b = v_new.astype(BF16)
            out = ws_qs[ch:] + _dot(intra, v_new_b)
            s_ref[hv] = state * cd + _dot_tn(k_dec, v_new_b)
            zc = z_ref[hv, 0, r0:r1, :].astype(F32)
            o_ref[hv, 0, r0:r1, :] = (_rms(out, nw_ref[...]) * _silu(zc)).astype(BF16)


def _gdn_core(proj, gates, conv_w, a_log, dt_bias, norm_w, n_kh):
    _, bsz, length, dk = proj.shape
    rows = GDN_ROWS
    smem = pl.BlockSpec(memory_space=pltpu.SMEM)
    return pl.pallas_call(
        _gdn_kernel,
        grid=(bsz, n_kh, length // rows),
        in_specs=[
            smem, smem,
            pl.BlockSpec((1, 1, rows, dk), lambda b, h, t: (h, b, t, 0)),
            pl.BlockSpec((1, 1, rows, dk), lambda b, h, t: (n_kh + h, b, t, 0)),
            pl.BlockSpec((2, 1, rows, dk), lambda b, h, t: (n_kh + h, b, t, 0)),
            pl.BlockSpec((2, 1, rows, dk), lambda b, h, t: (2 * n_kh + h, b, t, 0)),
            pl.BlockSpec((1, 2, 1, 2, rows), lambda b, h, t: (b, 0, h, 0, t)),
            pl.BlockSpec((GDN_CONV, 1, 1, dk), lambda b, h, t: (0, h, 0, 0)),
            pl.BlockSpec((GDN_CONV, 1, 1, dk), lambda b, h, t: (0, n_kh + h, 0, 0)),
            pl.BlockSpec((GDN_CONV, 2, 1, dk), lambda b, h, t: (0, n_kh + h, 0, 0)),
            pl.BlockSpec((1, dk), lambda b, h, t: (0, 0)),
        ],
        out_specs=pl.BlockSpec((2, 1, rows, dk), lambda b, h, t: (h, b, t, 0)),
        out_shape=jax.ShapeDtypeStruct((2 * n_kh, bsz, length, dk), BF16),
        scratch_shapes=[pltpu.VMEM((2, dk, dk), F32), pltpu.VMEM((4, SUBLANES, dk), F32)],
        compiler_params=_params("parallel", "parallel", "arbitrary"),
        name="gdn_core",
    )(a_log, dt_bias, proj, proj, proj, proj, gates, conv_w, conv_w, conv_w, norm_w.reshape(1, dk))


def _gdn_out_kernel(a_ref, w_ref, r_ref, o_ref, cat_ref):
    @pl.when(pl.program_id(1) == 0)
    def _():
        for hv in range(a_ref.shape[0]):
            cat_ref[:, hv * LANES:(hv + 1) * LANES] = a_ref[hv]

    o_ref[...] = r_ref[...] + _dot(cat_ref[...], w_ref[...])


def _gdn_out(o_heads, w, res):
    nh, t, dk = o_heads.shape
    n = w.shape[1]
    tm = _tile(t, 640, LANES)
    tn = _tile(n, 512, LANES)
    return pl.pallas_call(
        _gdn_out_kernel,
        grid=(t // tm, n // tn),
        in_specs=[
            pl.BlockSpec((nh, tm, dk), lambda i, j: (0, i, 0)),
            pl.BlockSpec((nh * dk, tn), lambda i, j: (0, j)),
            pl.BlockSpec((tm, tn), lambda i, j: (i, j)),
        ],
        out_specs=pl.BlockSpec((tm, tn), lambda i, j: (i, j)),
        out_shape=jax.ShapeDtypeStruct((t, n), F32),
        scratch_shapes=[pltpu.VMEM((tm, nh * dk), BF16)],
        compiler_params=_params("parallel", "arbitrary"),
        name="gdn_out_proj",
    )(o_heads, w, res)


def _final_norm_kernel(x_ref, w_ref, o_ref):
    o_ref[0] = _rms(x_ref[0], w_ref[...])


def _final_norm(h, w, front):
    bsz, length, d = h.shape
    tm = front
    nfront = front // tm
    return pl.pallas_call(
        _final_norm_kernel,
        grid=(bsz, (length - front) // tm),
        in_specs=[
            pl.BlockSpec((1, tm, d), lambda b, i: (b, i + nfront, 0)),
            pl.BlockSpec((1, d), lambda b, i: (0, 0)),
        ],
        out_specs=pl.BlockSpec((1, tm, d), lambda b, i: (b, i, 0)),
        out_shape=jax.ShapeDtypeStruct((bsz, length - front, d), F32),
        compiler_params=_params("parallel", "parallel"),
        name="final_norm",
    )(h, w.reshape(1, d))


def _conformer_layer(h, nw, w_pw1, b_pw1, w_dw, b_dw, ln_g, ln_b, w_pw2, b_pw2):
    bsz, length, d = h.shape
    u = _pw1_glu(h.reshape(bsz * length, d), nw, w_pw1.astype(BF16), b_pw1)
    return _dwconv_pw2(u.reshape(bsz, length, d), h, w_dw, b_dw, ln_g, ln_b, w_pw2.astype(BF16), b_pw2)


def _rope_tables(length, width):
    pos = (jnp.arange(length) - PAD_LEN).astype(F32)
    inv_freq = ROPE_THETA ** (-jnp.arange(0, HEAD_DIM, 2, dtype=F32) / HEAD_DIM)
    ang = pos[:, None] * inv_freq[None, :]
    cos, sin = jnp.cos(ang), jnp.sin(ang)
    reps = width // HEAD_DIM
    cos_t = jnp.tile(jnp.concatenate([cos, cos], axis=1), (1, reps))
    sin_t = jnp.tile(jnp.concatenate([-sin, sin], axis=1), (1, reps))
    return cos_t, sin_t


def _attention_layer(h, nw, w_qkv, b_qkv, sinks, w_o, b_o):
    bsz, length, d = h.shape
    n_heads = sinks.shape[0]
    qw = n_heads * HEAD_DIM
    kvw = (w_qkv.shape[1] - qw) // 2
    cos_t, sin_t = _rope_tables(length, kvw)
    hf = h.reshape(bsz * length, d)
    qkv = _qkv_rope(hf, nw, w_qkv.astype(BF16), b_qkv, cos_t, sin_t, length, qw, kvw)
    att = _attention(qkv.reshape(bsz, length, -1), sinks.astype(F32), n_heads, kvw // HEAD_DIM)
    out = _proj_res(att.reshape(bsz * length, qw), w_o.astype(BF16), b_o, hf)
    return out.reshape(bsz, length, d)


def _gdn_layer(h, nw, w_in, conv_w, a_log, dt_bias, norm_w, w_out):
    bsz, length, d = h.shape
    n_vh = a_log.shape[0]
    n_kh = n_vh // 2
    dk = GDN_DIM
    main = 2 * n_kh * dk + 2 * n_vh * dk
    hf = h.reshape(bsz * length, d)
    w_gate_t = jnp.zeros((LANES, d), F32).at[:2 * n_vh].set(w_in[:, main:].T).astype(BF16)
    proj, gate_t = _gdn_in(hf, nw, w_in[:, :main].astype(BF16), w_gate_t)
    proj = proj.reshape(main // dk, bsz, length, dk)
    gates = gate_t[:2 * n_vh].reshape(2, n_kh, 2, bsz, length).transpose(3, 0, 1, 2, 4)
    conv_w3 = conv_w.reshape(GDN_CONV, -1, 1, dk)
    o_heads = _gdn_core(proj, gates, conv_w3, a_log.astype(F32), dt_bias.astype(F32), norm_w, n_kh)
    out = _gdn_out(o_heads.reshape(n_vh, bsz * length, dk), w_out.astype(BF16), hf)
    return out.reshape(bsz, length, d)


def kernel(x, meta_tokens, norm_mix, norm_ffn, norm_final, conv_w_pw1, conv_b_pw1, conv_w_dw, conv_b_dw, conv_ln_g, conv_ln_b, conv_w_pw2, conv_b_pw2, attn_w_qkv, attn_b_qkv, attn_sinks, attn_w_o, attn_b_o, gdn_w_in, gdn_conv_w, gdn_a_log, gdn_dt_bias, gdn_norm_w, gdn_w_out, ffn_w_gate, ffn_w_up, ffn_w_down):
    bsz, seq, d = x.shape
    depth = norm_mix.shape[0]
    meta = jnp.broadcast_to(meta_tokens.astype(x.dtype)[None], (bsz, N_META, d))
    h = jnp.concatenate([jnp.zeros((bsz, PAD_LEN, d), x.dtype), meta, x], axis=1)
    length = h.shape[1]
    assert length % ATTN_BLOCK == 0 and length % GDN_ROWS == 0

    for i in range(depth):
        kind, j = i % 3, i // 3
        if kind == 0:
            h = _conformer_layer(h, norm_mix[i], conv_w_pw1[j], conv_b_pw1[j], conv_w_dw[j], conv_b_dw[j],
                                 conv_ln_g[j], conv_ln_b[j], conv_w_pw2[j], conv_b_pw2[j])
        elif kind == 1:
            h = _attention_layer(h, norm_mix[i], attn_w_qkv[j], attn_b_qkv[j], attn_sinks[j],
                                 attn_w_o[j], attn_b_o[j])
        else:
            h = _gdn_layer(h, norm_mix[i], gdn_w_in[j], gdn_conv_w[j], gdn_a_log[j], gdn_dt_bias[j],
                           gdn_norm_w[j], gdn_w_out[j])
        hf = _ffn(h.reshape(bsz * length, d), norm_ffn[i], ffn_w_gate[i].astype(BF16),
                  ffn_w_up[i].astype(BF16), ffn_w_down[i].astype(BF16))
        h = hf.reshape(bsz, length, d)

    return _final_norm(h, norm_final, FRONT)
```

```python
import functools
import math

import jax
import jax.numpy as jnp
from jax import lax
from jax.experimental import pallas as pl
from jax.experimental.pallas import tpu as pltpu

F32 = jnp.float32
BF16 = jnp.bfloat16

N_META = 16
FRONT = 128
PAD_LEN = FRONT - N_META
NORM_EPS = 1e-6
LN_EPS = 1e-5
NEG_INF = -1e30
CONV_KERNEL = 31
HEAD_DIM = 64
ATTN_BLOCK = 128
ROPE_THETA = 10000.0
GDN_DIM = 128
GDN_CONV = 4
GDN_CHUNK = 64

LANES = 128
SUBLANES = 8
VMEM_LIMIT_BYTES = 56 * 1024 * 1024

CONV_HALO = 32
GDN_ROWS = 2 * GDN_CHUNK
GDN_GROUP = 8


def _params(*sem):
    return pltpu.CompilerParams(dimension_semantics=sem, vmem_limit_bytes=VMEM_LIMIT_BYTES)


def _tile(n, target, mult):
    best = None
    for d in range(mult, min(n, target) + 1, mult):
        if n % d == 0:
            best = d
    assert best is not None, (n, target, mult)
    return best


def _rms(x, w):
    ms = jnp.mean(x * x, axis=-1, keepdims=True)
    return x * lax.rsqrt(ms + NORM_EPS) * w


def _sigmoid(x):
    return 1.0 / (1.0 + jnp.exp(-x))


def _silu(x):
    return x * _sigmoid(x)


def _dot(a, b):
    return jnp.dot(a, b, preferred_element_type=F32)


def _dot_nt(a, b):
    return lax.dot_general(a, b, (((1,), (1,)), ((), ())), preferred_element_type=F32)


def _dot_tn(a, b):
    return lax.dot_general(a, b, (((0,), (0,)), ((), ())), preferred_element_type=F32)


def _split3(x):
    hi = x.astype(BF16)
    r1 = x - hi.astype(F32)
    mid = r1.astype(BF16)
    lo = (r1 - mid.astype(F32)).astype(BF16)
    return hi, mid, lo


def _ffn_kernel(x_ref, nw_ref, wg_ref, wu_ref, wd_ref, o_ref, hn_ref, acc_ref):
    f = pl.program_id(1)

    @pl.when(f == 0)
    def _():
        hn_ref[...] = _rms(x_ref[...], nw_ref[...]).astype(BF16)
        acc_ref[...] = jnp.zeros_like(acc_ref)

    hn = hn_ref[...]
    g = _dot(hn, wg_ref[...])
    u = _dot(hn, wu_ref[...])
    a = (_silu(g) * u).astype(BF16)
    acc_ref[...] += _dot(a, wd_ref[...])

    @pl.when(f == pl.num_programs(1) - 1)
    def _():
        o_ref[...] = x_ref[...] + acc_ref[...]


def _ffn(h, nw, wg, wu, wd):
    t, d = h.shape
    f = wg.shape[1]
    tm = _tile(t, 640, LANES)
    tf = _tile(f, 512, LANES)
    return pl.pallas_call(
        _ffn_kernel,
        grid=(t // tm, f // tf),
        in_specs=[
            pl.BlockSpec((tm, d), lambda i, j: (i, 0)),
            pl.BlockSpec((1, d), lambda i, j: (0, 0)),
            pl.BlockSpec((d, tf), lambda i, j: (0, j)),
            pl.BlockSpec((d, tf), lambda i, j: (0, j)),
            pl.BlockSpec((tf, d), lambda i, j: (j, 0)),
        ],
        out_specs=pl.BlockSpec((tm, d), lambda i, j: (i, 0)),
        out_shape=jax.ShapeDtypeStruct((t, d), F32),
        scratch_shapes=[pltpu.VMEM((tm, d), BF16), pltpu.VMEM((tm, d), F32)],
        compiler_params=_params("parallel", "arbitrary"),
        name="ffn",
    )(h, nw.reshape(1, d), wg, wu, wd)


def _proj_res_kernel(a_ref, w_ref, b_ref, r_ref, o_ref):
    o_ref[...] = r_ref[...] + _dot(a_ref[...], w_ref[...]) + b_ref[...]


def _proj_res(a, w, b, res):
    t, k = a.shape
    n = w.shape[1]
    tm = _tile(t, 640, LANES)
    tn = _tile(n, 512, LANES)
    return pl.pallas_call(
        _proj_res_kernel,
        grid=(t // tm, n // tn),
        in_specs=[
            pl.BlockSpec((tm, k), lambda i, j: (i, 0)),
            pl.BlockSpec((k, tn), lambda i, j: (0, j)),
            pl.BlockSpec((1, tn), lambda i, j: (0, j)),
            pl.BlockSpec((tm, tn), lambda i, j: (i, j)),
        ],
        out_specs=pl.BlockSpec((tm, tn), lambda i, j: (i, j)),
        out_shape=jax.ShapeDtypeStruct((t, n), F32),
        compiler_params=_params("parallel", "arbitrary"),
        name="proj_res",
    )(a, w, b.reshape(1, n), res)


def _pw1_glu_kernel(x_ref, nw_ref, wv_ref, wg_ref, bv_ref, bg_ref, o_ref, hn_ref):
    @pl.when(pl.program_id(1) == 0)
    def _():
        hn_ref[...] = _rms(x_ref[...], nw_ref[...]).astype(BF16)

    hn = hn_ref[...]
    val = _dot(hn, wv_ref[...]) + bv_ref[...]
    gate = _dot(hn, wg_ref[...]) + bg_ref[...]
    o_ref[...] = val * _sigmoid(gate)


def _pw1_glu(h, nw, w, b):
    t, d = h.shape
    tm = _tile(t, 640, LANES)
    tn = _tile(d, 512, LANES)
    nj = d // tn
    b2 = b.reshape(1, 2 * d)
    return pl.pallas_call(
        _pw1_glu_kernel,
        grid=(t // tm, nj),
        in_specs=[
            pl.BlockSpec((tm, d), lambda i, j: (i, 0)),
            pl.BlockSpec((1, d), lambda i, j: (0, 0)),
            pl.BlockSpec((d, tn), lambda i, j: (0, j)),
            pl.BlockSpec((d, tn), lambda i, j: (0, j + nj)),
            pl.BlockSpec((1, tn), lambda i, j: (0, j)),
            pl.BlockSpec((1, tn), lambda i, j: (0, j + nj)),
        ],
        out_specs=pl.BlockSpec((tm, tn), lambda i, j: (i, j)),
        out_shape=jax.ShapeDtypeStruct((t, d), F32),
        scratch_shapes=[pltpu.VMEM((tm, d), BF16)],
        compiler_params=_params("parallel", "arbitrary"),
        name="conv_pw1_glu",
    )(h, nw.reshape(1, d), w, w, b2, b2)


def _dwconv_pw2_kernel(uc_ref, up_ref, wdw_ref, bdw_ref, lng_ref, lnb_ref, w2_ref, b2_ref,
                       r_ref, o_ref, xx_ref, y_ref, a_ref, *, tm, d):
    li = pl.program_id(1)
    rb = 64
    ncb = d // LANES

    @pl.when(pl.program_id(2) == 0)
    def _():
        pos = li * tm + lax.broadcasted_iota(jnp.int32, (tm, 1), 0)
        ppos = li * tm - CONV_HALO + lax.broadcasted_iota(jnp.int32, (CONV_HALO, 1), 0)
        for cb in range(ncb):
            cs = slice(cb * LANES, (cb + 1) * LANES)
            xx_ref[cb, 0:CONV_HALO, :] = jnp.where(ppos >= PAD_LEN, up_ref[0, :, cs], 0.0)
            xx_ref[cb, CONV_HALO:, :] = jnp.where(pos >= PAD_LEN, uc_ref[0, :, cs], 0.0)

        base = CONV_HALO - CONV_KERNEL + 1

        def col_block(cb, carry):
            for r in range(tm // rb):
                acc = jnp.zeros((rb, LANES), F32)
                for k in range(CONV_KERNEL):
                    r0 = r * rb + base + k
                    acc = acc + wdw_ref[cb, k:k + 1, :] * xx_ref[cb, r0:r0 + rb, :]
                y_ref[cb, r * rb:(r + 1) * rb, :] = acc + bdw_ref[cb]
            return carry

        lax.fori_loop(0, ncb, col_block, 0)

        total = y_ref[0]
        for cb in range(1, ncb):
            total = total + y_ref[cb]
        mu = jnp.sum(total, axis=-1, keepdims=True) * (1.0 / d)
        sq = jnp.zeros((tm, LANES), F32)
        for cb in range(ncb):
            yc = y_ref[cb] - mu
            sq = sq + yc * yc
        rstd = lax.rsqrt(jnp.sum(sq, axis=-1, keepdims=True) * (1.0 / d) + LN_EPS)
        for cb in range(ncb):
            ln = (y_ref[cb] - mu) * rstd * lng_ref[cb] + lnb_ref[cb]
            a_ref[:, cb * LANES:(cb + 1) * LANES] = _silu(ln).astype(BF16)

    o_ref[0] = r_ref[0] + _dot(a_ref[...], w2_ref[...]) + b2_ref[...]


def _dwconv_pw2(u, h, wdw, bdw, lng, lnb, w2, b2):
    bsz, length, d = u.shape
    tm = _tile(length, 640, 64)
    assert tm % CONV_HALO == 0
    tn = _tile(d, 512, LANES)
    halo_blocks = tm // CONV_HALO
    ncb = d // LANES
    wdw_p = jnp.zeros((CONV_HALO, d), F32).at[:CONV_KERNEL].set(wdw)
    wdw_p = wdw_p.reshape(CONV_HALO, ncb, LANES).transpose(1, 0, 2)
    row = lambda v: v.reshape(ncb, 1, LANES)
    whole = lambda shape: pl.BlockSpec(shape, lambda b, l, j: (0,) * len(shape))
    kern = functools.partial(_dwconv_pw2_kernel, tm=tm, d=d)
    return pl.pallas_call(
        kern,
        grid=(bsz, length // tm, d // tn),
        in_specs=[
            pl.BlockSpec((1, tm, d), lambda b, l, j: (b, l, 0)),
            pl.BlockSpec((1, CONV_HALO, d), lambda b, l, j: (b, jnp.maximum(l * halo_blocks - 1, 0), 0)),
            whole((ncb, CONV_HALO, LANES)),
            whole((ncb, 1, LANES)),
            whole((ncb, 1, LANES)),
            whole((ncb, 1, LANES)),
            pl.BlockSpec((d, tn), lambda b, l, j: (0, j)),
            pl.BlockSpec((1, tn), lambda b, l, j: (0, j)),
            pl.BlockSpec((1, tm, tn), lambda b, l, j: (b, l, j)),
        ],
        out_specs=pl.BlockSpec((1, tm, tn), lambda b, l, j: (b, l, j)),
        out_shape=jax.ShapeDtypeStruct((bsz, length, d), F32),
        scratch_shapes=[
            pltpu.VMEM((ncb, CONV_HALO + tm, LANES), F32),
            pltpu.VMEM((ncb, tm, LANES), F32),
            pltpu.VMEM((tm, d), BF16),
        ],
        compiler_params=_params("parallel", "parallel", "arbitrary"),
        name="conv_dw_pw2",
    )(u, u, wdw_p, row(bdw), row(lng), row(lnb), w2, b2.reshape(1, d), h)


def _qkv_rope_kernel(x_ref, nw_ref, w_ref, b_ref, cos_ref, sin_ref, o_ref, hn_ref, *, n_q, n_rope):
    j = pl.program_id(1)

    @pl.when(j == 0)
    def _():
        hn_ref[...] = _rms(x_ref[...], nw_ref[...]).astype(BF16)

    acc = _dot(hn_ref[...], w_ref[...]) + b_ref[...]
    tn = acc.shape[1]

    @pl.when(j < n_rope)
    def _():
        lane = lax.broadcasted_iota(jnp.int32, acc.shape, 1)
        first = (lane & (HEAD_DIM - 1)) < (HEAD_DIM // 2)
        rot = jnp.where(first, pltpu.roll(acc, tn - HEAD_DIM // 2, axis=1),
                        pltpu.roll(acc, HEAD_DIM // 2, axis=1))
        r = acc * cos_ref[...] + rot * sin_ref[...]
        scale = jnp.where(j < n_q, HEAD_DIM ** -0.5, 1.0)
        o_ref[...] = (r * scale).astype(BF16)

    @pl.when(j >= n_rope)
    def _():
        o_ref[...] = acc.astype(BF16)


def _qkv_rope(h, nw, w, b, cos_t, sin_t, length, q_width, kv_width):
    t, d = h.shape
    n = w.shape[1]
    tn = kv_width
    assert tn % LANES == 0 and q_width % tn == 0 and n == q_width + 2 * kv_width
    tm = _tile(length, 640, LANES)
    nl = length // tm
    kern = functools.partial(_qkv_rope_kernel, n_q=q_width // tn, n_rope=q_width // tn + 1)
    return pl.pallas_call(
        kern,
        grid=(t // tm, n // tn),
        in_specs=[
            pl.BlockSpec((tm, d), lambda i, j: (i, 0)),
            pl.BlockSpec((1, d), lambda i, j: (0, 0)),
            pl.BlockSpec((d, tn), lambda i, j: (0, j)),
            pl.BlockSpec((1, tn), lambda i, j: (0, j)),
            pl.BlockSpec((tm, tn), lambda i, j: (i % nl, 0)),
            pl.BlockSpec((tm, tn), lambda i, j: (i % nl, 0)),
        ],
        out_specs=pl.BlockSpec((tm, tn), lambda i, j: (i, j)),
        out_shape=jax.ShapeDtypeStruct((t, n), BF16),
        scratch_shapes=[pltpu.VMEM((tm, d), BF16)],
        compiler_params=_params("parallel", "arbitrary"),
        name="attn_qkv_rope",
    )(h, nw.reshape(1, d), w, b.reshape(1, n), cos_t, sin_t)


def _attn_kernel(sink_ref, q_ref, kc_ref, kp_ref, vc_ref, vp_ref, o_ref, *, n_heads, n_kv):
    n = pl.program_id(1)
    blk = ATTN_BLOCK
    group = n_heads // n_kv
    qi = lax.broadcasted_iota(jnp.int32, (blk, 2 * blk), 0)
    ks = lax.broadcasted_iota(jnp.int32, (blk, 2 * blk), 1)
    k_idx = (n - 1) * blk + ks
    allowed = (ks > qi) & (ks <= qi + blk) & (k_idx >= PAD_LEN)
    head_cols = lambda h: slice(h * HEAD_DIM, (h + 1) * HEAD_DIM)
    kb = [jnp.concatenate([kp_ref[0, :, head_cols(kv)], kc_ref[0, :, head_cols(kv)]], axis=0)
          for kv in range(n_kv)]
    vb = [jnp.concatenate([vp_ref[0, :, head_cols(kv)], vc_ref[0, :, head_cols(kv)]], axis=0)
          for kv in range(n_kv)]
    for h0 in range(0, n_heads, group):
        heads = range(h0, h0 + group)
        sc = [jnp.where(allowed, _dot_nt(q_ref[0, :, head_cols(h)], kb[h // group]), NEG_INF) for h in heads]
        probs = []
        for h, s in zip(heads, sc):
            sink = sink_ref[h]
            m = jnp.maximum(jnp.max(s, axis=-1, keepdims=True), sink)
            e = jnp.exp(s - m)
            denom = jnp.sum(e, axis=-1, keepdims=True) + jnp.exp(sink - m)
            probs.append((e / denom).astype(BF16))
        outs = [_dot(p, vb[h // group]) for h, p in zip(heads, probs)]
        for h, o in zip(heads, outs):
            o_ref[0, :, head_cols(h)] = o.astype(BF16)


def _attention(qkv, sinks, n_heads, n_kv):
    bsz, length, _ = qkv.shape
    blk = ATTN_BLOCK
    qw = n_heads * HEAD_DIM
    kvw = n_kv * HEAD_DIM
    kcol = qw // kvw
    kern = functools.partial(_attn_kernel, n_heads=n_heads, n_kv=n_kv)
    prev = lambda n: jnp.maximum(n - 1, 0)
    return pl.pallas_call(
        kern,
        grid=(bsz, length // blk),
        in_specs=[
            pl.BlockSpec(memory_space=pltpu.SMEM),
            pl.BlockSpec((1, blk, qw), lambda b, n: (b, n, 0)),
            pl.BlockSpec((1, blk, kvw), lambda b, n: (b, n, kcol)),
            pl.BlockSpec((1, blk, kvw), lambda b, n: (b, prev(n), kcol)),
            pl.BlockSpec((1, blk, kvw), lambda b, n: (b, n, kcol + 1)),
            pl.BlockSpec((1, blk, kvw), lambda b, n: (b, prev(n), kcol + 1)),
        ],
        out_specs=pl.BlockSpec((1, blk, qw), lambda b, n: (b, n, 0)),
        out_shape=jax.ShapeDtypeStruct((bsz, length, qw), BF16),
        compiler_params=_params("parallel", "arbitrary"),
        name="attn_core",
    )(sinks, qkv, qkv, qkv, qkv, qkv)


def _gdn_in_kernel(x_ref, nw_ref, w_ref, wg_ref, o_ref, g_ref, hn_ref):
    @pl.when(pl.program_id(1) == 0)
    def _():
        hn_ref[...] = _rms(x_ref[...], nw_ref[...]).astype(BF16)
        g_ref[...] = _dot_nt(wg_ref[...], hn_ref[...])

    acc = _dot(hn_ref[...], w_ref[...])
    for c in range(o_ref.shape[0]):
        o_ref[c] = acc[:, c * LANES:(c + 1) * LANES].astype(BF16)


def _gdn_in(h, nw, w_main, w_gate_t):
    t, d = h.shape
    n = w_main.shape[1]
    tm = _tile(t, 640, LANES)
    tn = _tile(n, 512, LANES)
    cpb = tn // LANES
    return pl.pallas_call(
        _gdn_in_kernel,
        grid=(t // tm, n // tn),
        in_specs=[
            pl.BlockSpec((tm, d), lambda i, j: (i, 0)),
            pl.BlockSpec((1, d), lambda i, j: (0, 0)),
            pl.BlockSpec((d, tn), lambda i, j: (0, j)),
            pl.BlockSpec((LANES, d), lambda i, j: (0, 0)),
        ],
        out_specs=[
            pl.BlockSpec((cpb, tm, LANES), lambda i, j: (j, i, 0)),
            pl.BlockSpec((LANES, tm), lambda i, j: (0, i)),
        ],
        out_shape=[
            jax.ShapeDtypeStruct((n // LANES, t, LANES), BF16),
            jax.ShapeDtypeStruct((LANES, t), F32),
        ],
        scratch_shapes=[pltpu.VMEM((tm, d), BF16)],
        compiler_params=_params("parallel", "arbitrary"),
        name="gdn_in_proj",
    )(h, nw.reshape(1, d), w_main, w_gate_t)


def _gdn_kernel(alog_ref, dtb_ref, q_ref, k_ref, v_ref, z_ref, gate_ref, cwq_ref, cwk_ref, cwv_ref,
                nw_ref, o_ref, s_ref, tail_ref):
    ti = pl.program_id(2)
    rows = GDN_ROWS
    ch = GDN_CHUNK
    dk = GDN_DIM
    n_grp = q_ref.shape[0]

    @pl.when(ti == 0)
    def _():
        s_ref[...] = jnp.zeros_like(s_ref)
        tail_ref[...] = jnp.zeros_like(tail_ref)

    pos_col = ti * rows + lax.broadcasted_iota(jnp.int32, (rows, 1), 0)
    valid_col = pos_col >= PAD_LEN

    tails = [tail_ref[s] for s in range(4 * n_grp)]
    new_tails = [None] * (4 * n_grp)

    def conv_silu(x_raw, slot, w_ref, wi):
        x = jnp.where(valid_col, x_raw.astype(F32), 0.0)
        xx = jnp.concatenate([tails[slot], x], axis=0)
        new_tails[slot] = x[rows - SUBLANES:, :]
        y = w_ref[GDN_CONV - 1, wi] * x
        for k in range(GDN_CONV - 1):
            shifted = pltpu.roll(xx, GDN_CONV - 1 - k, axis=0)
            y = y + w_ref[k, wi] * shifted[SUBLANES:, :]
        return _silu(y)

    def l2n(x):
        return x * lax.rsqrt(jnp.sum(x * x, axis=-1, keepdims=True) + 1e-6)

    qs, ks, vs = [], [], []
    for g in range(n_grp):
        qs.append(l2n(conv_silu(q_ref[g, 0], 4 * g, cwq_ref, g)) * (dk ** -0.5))
        ks.append(l2n(conv_silu(k_ref[g, 0], 4 * g + 1, cwk_ref, g)))
        vs.append(conv_silu(v_ref[2 * g, 0], 4 * g + 2, cwv_ref, 2 * g))
        vs.append(conv_silu(v_ref[2 * g + 1, 0], 4 * g + 3, cwv_ref, 2 * g + 1))
    for s in range(4 * n_grp):
        tail_ref[s] = new_tails[s]

    n_vh = 2 * n_grp
    b_log = gate_ref[0, 0, 0]
    a_log = gate_ref[0, 1, 0]
    pos_row = ti * rows + lax.broadcasted_iota(jnp.int32, (n_vh, rows), 1)
    valid_row = pos_row >= PAD_LEN
    xs = a_log + dtb_ref[0]
    softplus = jnp.maximum(xs, 0.0) + jnp.log1p(jnp.exp(-jnp.abs(xs)))
    g_row = jnp.where(valid_row, -jnp.exp(alog_ref[0]) * softplus, 0.0)
    beta_row = jnp.where(valid_row, _sigmoid(b_log), 0.0)

    zpad = jnp.zeros((LANES - n_vh, rows), F32)
    g_pad = jnp.concatenate([g_row, zpad], axis=0)
    beta_pad = jnp.concatenate([beta_row, zpad], axis=0)
    ri = lax.broadcasted_iota(jnp.int32, (rows, rows), 0)
    ci = lax.broadcasted_iota(jnp.int32, (rows, rows), 1)
    shift = int(math.log2(ch))
    same_chunk = jnp.right_shift(ri, shift) == jnp.right_shift(ci, shift)
    upper = jnp.where(same_chunk & (ri <= ci), 1.0, 0.0).astype(BF16)
    lower = jnp.where(same_chunk & (ci <= ri), 1.0, 0.0).astype(BF16)
    eye = jnp.where(ri == ci, 1.0, 0.0).astype(BF16)
    g3 = _split3(g_pad)
    b3 = _split3(beta_pad)
    gc_row = _dot(g3[0], upper) + _dot(g3[1], upper) + _dot(g3[2], upper)
    gc_col = _dot_nt(lower, g3[0]) + _dot_nt(lower, g3[1]) + _dot_nt(lower, g3[2])
    beta_col = _dot_nt(eye, b3[0]) + _dot_nt(eye, b3[1]) + _dot_nt(eye, b3[2])

    ii = lax.broadcasted_iota(jnp.int32, (ch, ch), 0)
    jj = lax.broadcasted_iota(jnp.int32, (ch, ch), 1)
    causal = ii >= jj
    strict = ii > jj
    ident = jnp.where(ii == jj, 1.0, 0.0)

    n_ch = rows // ch
    pairs = [(c, hv) for c in range(n_ch) for hv in range(n_vh)]
    span = lambda c: slice(c * ch, (c + 1) * ch)
    kk, qk = {}, {}
    for c in range(n_ch):
        for g in range(n_grp):
            kc_b = ks[g][span(c)].astype(BF16)
            kk[c, g] = _dot_nt(kc_b, kc_b)
            qk[c, g] = _dot_nt(qs[g][span(c)].astype(BF16), kc_b)
    gcol, glast, bcol, decay, x, p = {}, {}, {}, {}, {}, {}
    for c, hv in pairs:
        gcol[c, hv] = gc_col[span(c), hv:hv + 1]
        grow = gc_row[hv:hv + 1, span(c)]
        glast[c, hv] = gc_col[(c + 1) * ch - 1:(c + 1) * ch, hv:hv + 1]
        bcol[c, hv] = beta_col[span(c), hv:hv + 1]
        decay[c, hv] = jnp.where(causal, jnp.exp(gcol[c, hv] - grow), 0.0)
        lm = jnp.where(strict, kk[c, hv // 2] * bcol[c, hv] * decay[c, hv], 0.0)
        x[c, hv] = ident - lm
        lm_b = lm.astype(BF16)
        p[c, hv] = _dot(lm_b, lm_b)
    for r in range(5):
        for key in pairs:
            p_b = p[key].astype(BF16)
            x[key] = x[key] + _dot(x[key].astype(BF16), p_b)
            if r < 4:
                p[key] = _dot(p_b, p_b)
    u, w, egc = {}, {}, {}
    for c, hv in pairs:
        kc = ks[hv // 2][span(c)]
        egc[c, hv] = jnp.exp(gcol[c, hv])
        rhs = jnp.concatenate([vs[hv][span(c)] * bcol[c, hv], kc * bcol[c, hv] * egc[c, hv]], axis=1)
        sol = _dot(x[c, hv].astype(BF16), rhs.astype(BF16))
        u[c, hv] = sol[:, :dk]
        w[c, hv] = sol[:, dk:]
    states = [s_ref[hv] for hv in range(n_vh)]
    for c in range(n_ch):
        ws_qs, v_new_b, outs = {}, {}, {}
        for hv in range(n_vh):
            q_dec = qs[hv // 2][span(c)] * egc[c, hv]
            wq = jnp.concatenate([w[c, hv], q_dec], axis=0).astype(BF16)
            ws_qs[hv] = _dot(wq, states[hv].astype(BF16))
        for hv in range(n_vh):
            v_new_b[hv] = (u[c, hv] - ws_qs[hv][:ch]).astype(BF16)
            intra = jnp.where(causal, qk[c, hv // 2] * decay[c, hv], 0.0).astype(BF16)
            outs[hv] = ws_qs[hv][ch:] + _dot(intra, v_new_b[hv])
        for hv in range(n_vh):
            k_dec = (ks[hv // 2][span(c)] * jnp.exp(glast[c, hv] - gcol[c, hv])).astype(BF16)
            states[hv] = states[hv] * jnp.exp(glast[c, hv]) + _dot_tn(k_dec, v_new_b[hv])
        for hv in range(n_vh):
            zc = z_ref[hv, 0, span(c), :].astype(F32)
            o_ref[hv, 0, span(c), :] = (_rms(outs[hv], nw_ref[...]) * _silu(zc)).astype(BF16)
    for hv in range(n_vh):
        s_ref[hv] = states[hv]


def _gdn_core(proj, gates, conv_w, a_log, dt_bias, norm_w, n_kh, grp):
    _, bsz, length, dk = proj.shape
    rows = GDN_ROWS
    ng = n_kh // grp
    return pl.pallas_call(
        _gdn_kernel,
        grid=(bsz, ng, length // rows),
        in_specs=[
            pl.BlockSpec((1, 2 * grp, rows), lambda b, h, t: (h, 0, 0)),
            pl.BlockSpec((1, 2 * grp, rows), lambda b, h, t: (h, 0, 0)),
            pl.BlockSpec((grp, 1, rows, dk), lambda b, h, t: (h, b, t, 0)),
            pl.BlockSpec((grp, 1, rows, dk), lambda b, h, t: (ng + h, b, t, 0)),
            pl.BlockSpec((2 * grp, 1, rows, dk), lambda b, h, t: (ng + h, b, t, 0)),
            pl.BlockSpec((2 * grp, 1, rows, dk), lambda b, h, t: (2 * ng + h, b, t, 0)),
            pl.BlockSpec((1, 2, 1, 2 * grp, rows), lambda b, h, t: (b, 0, h, 0, t)),
            pl.BlockSpec((GDN_CONV, grp, 1, dk), lambda b, h, t: (0, h, 0, 0)),
            pl.BlockSpec((GDN_CONV, grp, 1, dk), lambda b, h, t: (0, ng + h, 0, 0)),
            pl.BlockSpec((GDN_CONV, 2 * grp, 1, dk), lambda b, h, t: (0, ng + h, 0, 0)),
            pl.BlockSpec((1, dk), lambda b, h, t: (0, 0)),
        ],
        out_specs=pl.BlockSpec((2 * grp, 1, rows, dk), lambda b, h, t: (h, b, t, 0)),
        out_shape=jax.ShapeDtypeStruct((2 * n_kh, bsz, length, dk), BF16),
        scratch_shapes=[pltpu.VMEM((2 * grp, dk, dk), F32), pltpu.VMEM((4 * grp, SUBLANES, dk), F32)],
        compiler_params=_params("parallel", "parallel", "arbitrary"),
        name="gdn_core",
    )(a_log, dt_bias, proj, proj, proj, proj, gates, conv_w, conv_w, conv_w, norm_w.reshape(1, dk))


def _gdn_out_kernel(a_ref, w_ref, r_ref, o_ref, cat_ref):
    @pl.when(pl.program_id(1) == 0)
    def _():
        for hv in range(a_ref.shape[0]):
            cat_ref[:, hv * LANES:(hv + 1) * LANES] = a_ref[hv]

    o_ref[...] = r_ref[...] + _dot(cat_ref[...], w_ref[...])


def _gdn_out(o_heads, w, res):
    nh, t, dk = o_heads.shape
    n = w.shape[1]
    tm = _tile(t, 640, LANES)
    tn = _tile(n, 512, LANES)
    return pl.pallas_call(
        _gdn_out_kernel,
        grid=(t // tm, n // tn),
        in_specs=[
            pl.BlockSpec((nh, tm, dk), lambda i, j: (0, i, 0)),
            pl.BlockSpec((nh * dk, tn), lambda i, j: (0, j)),
            pl.BlockSpec((tm, tn), lambda i, j: (i, j)),
        ],
        out_specs=pl.BlockSpec((tm, tn), lambda i, j: (i, j)),
        out_shape=jax.ShapeDtypeStruct((t, n), F32),
        scratch_shapes=[pltpu.VMEM((tm, nh * dk), BF16)],
        compiler_params=_params("parallel", "arbitrary"),
        name="gdn_out_proj",
    )(o_heads, w, res)


def _final_norm_kernel(x_ref, w_ref, o_ref):
    o_ref[0] = _rms(x_ref[0], w_ref[...])


def _final_norm(h, w, front):
    bsz, length, d = h.shape
    tm = front
    nfront = front // tm
    return pl.pallas_call(
        _final_norm_kernel,
        grid=(bsz, (length - front) // tm),
        in_specs=[
            pl.BlockSpec((1, tm, d), lambda b, i: (b, i + nfront, 0)),
            pl.BlockSpec((1, d), lambda b, i: (0, 0)),
        ],
        out_specs=pl.BlockSpec((1, tm, d), lambda b, i: (b, i, 0)),
        out_shape=jax.ShapeDtypeStruct((bsz, length - front, d), F32),
        compiler_params=_params("parallel", "parallel"),
        name="final_norm",
    )(h, w.reshape(1, d))


def _conformer_layer(h, nw, w_pw1, b_pw1, w_dw, b_dw, ln_g, ln_b, w_pw2, b_pw2):
    bsz, length, d = h.shape
    u = _pw1_glu(h.reshape(bsz * length, d), nw, w_pw1.astype(BF16), b_pw1)
    return _dwconv_pw2(u.reshape(bsz, length, d), h, w_dw, b_dw, ln_g, ln_b, w_pw2.astype(BF16), b_pw2)


def _rope_tables(length, width):
    pos = (jnp.arange(length) - PAD_LEN).astype(F32)
    inv_freq = ROPE_THETA ** (-jnp.arange(0, HEAD_DIM, 2, dtype=F32) / HEAD_DIM)
    ang = pos[:, None] * inv_freq[None, :]
    cos, sin = jnp.cos(ang), jnp.sin(ang)
    reps = width // HEAD_DIM
    cos_t = jnp.tile(jnp.concatenate([cos, cos], axis=1), (1, reps))
    sin_t = jnp.tile(jnp.concatenate([-sin, sin], axis=1), (1, reps))
    return cos_t, sin_t


def _attention_layer(h, nw, w_qkv, b_qkv, sinks, w_o, b_o):
    bsz, length, d = h.shape
    n_heads = sinks.shape[0]
    qw = n_heads * HEAD_DIM
    kvw = (w_qkv.shape[1] - qw) // 2
    cos_t, sin_t = _rope_tables(length, kvw)
    hf = h.reshape(bsz * length, d)
    qkv = _qkv_rope(hf, nw, w_qkv.astype(BF16), b_qkv, cos_t, sin_t, length, qw, kvw)
    att = _attention(qkv.reshape(bsz, length, -1), sinks.astype(F32), n_heads, kvw // HEAD_DIM)
    out = _proj_res(att.reshape(bsz * length, qw), w_o.astype(BF16), b_o, hf)
    return out.reshape(bsz, length, d)


def _gdn_layer(h, nw, w_in, conv_w, a_log, dt_bias, norm_w, w_out):
    bsz, length, d = h.shape
    n_vh = a_log.shape[0]
    n_kh = n_vh // 2
    dk = GDN_DIM
    main = 2 * n_kh * dk + 2 * n_vh * dk
    hf = h.reshape(bsz * length, d)
    w_gate_t = jnp.zeros((LANES, d), F32).at[:2 * n_vh].set(w_in[:, main:].T).astype(BF16)
    proj, gate_t = _gdn_in(hf, nw, w_in[:, :main].astype(BF16), w_gate_t)
    proj = proj.reshape(main // dk, bsz, length, dk)
    grp = GDN_GROUP if n_kh % GDN_GROUP == 0 else 1
    ng = n_kh // grp
    gates = gate_t[:2 * n_vh].reshape(2, ng, 2 * grp, bsz, length).transpose(3, 0, 1, 2, 4)
    conv_w3 = conv_w.reshape(GDN_CONV, -1, 1, dk)
    per_head = lambda v: jnp.broadcast_to(v.astype(F32).reshape(ng, 2 * grp, 1), (ng, 2 * grp, GDN_ROWS))
    o_heads = _gdn_core(proj, gates, conv_w3, per_head(a_log), per_head(dt_bias), norm_w, n_kh, grp)
    out = _gdn_out(o_heads.reshape(n_vh, bsz * length, dk), w_out.astype(BF16), hf)
    return out.reshape(bsz, length, d)


def kernel(x, meta_tokens, norm_mix, norm_ffn, norm_final, conv_w_pw1, conv_b_pw1, conv_w_dw, conv_b_dw, conv_ln_g, conv_ln_b, conv_w_pw2, conv_b_pw2, attn_w_qkv, attn_b_qkv, attn_sinks, attn_w_o, attn_b_o, gdn_w_in, gdn_conv_w, gdn_a_log, gdn_dt_bias, gdn_norm_w, gdn_w_out, ffn_w_gate, ffn_w_up, ffn_w_down):
    bsz, seq, d = x.shape
    depth = norm_mix.shape[0]
    meta = jnp.broadcast_to(meta_tokens.astype(x.dtype)[None], (bsz, N_META, d))
    h = jnp.concatenate([jnp.zeros((bsz, PAD_LEN, d), x.dtype), meta, x], axis=1)
    length = h.shape[1]
    assert length % ATTN_BLOCK == 0 and length % GDN_ROWS == 0

    for i in range(depth):
        kind, j = i % 3, i // 3
        if kind == 0:
            h = _conformer_layer(h, norm_mix[i], conv_w_pw1[j], conv_b_pw1[j], conv_w_dw[j], conv_b_dw[j],
                                 conv_ln_g[j], conv_ln_b[j], conv_w_pw2[j], conv_b_pw2[j])
        elif kind == 1:
            h = _attention_layer(h, norm_mix[i], attn_w_qkv[j], attn_b_qkv[j], attn_sinks[j],
                                 attn_w_o[j], attn_b_o[j])
        else:
            h = _gdn_layer(h, norm_mix[i], gdn_w_in[j], gdn_conv_w[j], gdn_a_log[j], gdn_dt_bias[j],
                           gdn_norm_w[j], gdn_w_out[j])
        hf = _ffn(h.reshape(bsz * length, d), norm_ffn[i], ffn_w_gate[i].astype(BF16),
                  ffn_w_up[i].astype(BF16), ffn_w_down[i].astype(BF16))
        h = hf.reshape(bsz, length, d)

    return _final_norm(h, norm_final, FRONT)
```

```python
import functools
import math

import jax
import jax.numpy as jnp
from jax import lax
from jax.experimental import pallas as pl
from jax.experimental.pallas import tpu as pltpu

F32 = jnp.float32
BF16 = jnp.bfloat16

N_META = 16
FRONT = 128
PAD_LEN = FRONT - N_META
NORM_EPS = 1e-6
LN_EPS = 1e-5
NEG_INF = -1e30
CONV_KERNEL = 31
HEAD_DIM = 64
ATTN_BLOCK = 128
ROPE_THETA = 10000.0
GDN_DIM = 128
GDN_CONV = 4
GDN_CHUNK = 64

LANES = 128
SUBLANES = 8
VMEM_LIMIT_BYTES = 56 * 1024 * 1024

CONV_HALO = 32
GDN_ROWS = 2 * GDN_CHUNK
GDN_GROUP = 8


def _params(*sem):
    return pltpu.CompilerParams(dimension_semantics=sem, vmem_limit_bytes=VMEM_LIMIT_BYTES)


def _tile(n, target, mult):
    best = None
    for d in range(mult, min(n, target) + 1, mult):
        if n % d == 0:
            best = d
    assert best is not None, (n, target, mult)
    return best


def _rms(x, w):
    ms = jnp.mean(x * x, axis=-1, keepdims=True)
    return x * lax.rsqrt(ms + NORM_EPS) * w


def _sigmoid(x):
    return 1.0 / (1.0 + jnp.exp(-x))


def _silu(x):
    return x * _sigmoid(x)


def _dot(a, b):
    return jnp.dot(a, b, preferred_element_type=F32)


def _dot_nt(a, b):
    return lax.dot_general(a, b, (((1,), (1,)), ((), ())), preferred_element_type=F32)


def _dot_tn(a, b):
    return lax.dot_general(a, b, (((0,), (0,)), ((), ())), preferred_element_type=F32)


def _split3(x):
    hi = x.astype(BF16)
    r1 = x - hi.astype(F32)
    mid = r1.astype(BF16)
    lo = (r1 - mid.astype(F32)).astype(BF16)
    return hi, mid, lo


def _ffn_kernel(x_ref, nw_ref, wg_ref, wu_ref, wd_ref, o_ref, hn_ref, acc_ref):
    f = pl.program_id(1)

    @pl.when(f == 0)
    def _():
        hn_ref[...] = _rms(x_ref[...], nw_ref[...]).astype(BF16)
        acc_ref[...] = jnp.zeros_like(acc_ref)

    hn = hn_ref[...]
    g = _dot(hn, wg_ref[...])
    u = _dot(hn, wu_ref[...])
    a = (_silu(g) * u).astype(BF16)
    acc_ref[...] += _dot(a, wd_ref[...])

    @pl.when(f == pl.num_programs(1) - 1)
    def _():
        o_ref[...] = x_ref[...] + acc_ref[...]


def _ffn(h, nw, wg, wu, wd):
    t, d = h.shape
    f = wg.shape[1]
    tm = _tile(t, 640, LANES)
    tf = _tile(f, 512, LANES)
    return pl.pallas_call(
        _ffn_kernel,
        grid=(t // tm, f // tf),
        in_specs=[
            pl.BlockSpec((tm, d), lambda i, j: (i, 0)),
            pl.BlockSpec((1, d), lambda i, j: (0, 0)),
            pl.BlockSpec((d, tf), lambda i, j: (0, j)),
            pl.BlockSpec((d, tf), lambda i, j: (0, j)),
            pl.BlockSpec((tf, d), lambda i, j: (j, 0)),
        ],
        out_specs=pl.BlockSpec((tm, d), lambda i, j: (i, 0)),
        out_shape=jax.ShapeDtypeStruct((t, d), F32),
        scratch_shapes=[pltpu.VMEM((tm, d), BF16), pltpu.VMEM((tm, d), F32)],
        compiler_params=_params("parallel", "arbitrary"),
        name="ffn",
    )(h, nw.reshape(1, d), wg, wu, wd)


def _proj_res_kernel(a_ref, w_ref, b_ref, r_ref, o_ref):
    o_ref[...] = r_ref[...] + _dot(a_ref[...], w_ref[...]) + b_ref[...]


def _proj_res(a, w, b, res):
    t, k = a.shape
    n = w.shape[1]
    tm = _tile(t, 1280, LANES)
    tn = _tile(n, 512, LANES)
    return pl.pallas_call(
        _proj_res_kernel,
        grid=(t // tm, n // tn),
        in_specs=[
            pl.BlockSpec((tm, k), lambda i, j: (i, 0)),
            pl.BlockSpec((k, tn), lambda i, j: (0, j)),
            pl.BlockSpec((1, tn), lambda i, j: (0, j)),
            pl.BlockSpec((tm, tn), lambda i, j: (i, j)),
        ],
        out_specs=pl.BlockSpec((tm, tn), lambda i, j: (i, j)),
        out_shape=jax.ShapeDtypeStruct((t, n), F32),
        compiler_params=_params("parallel", "arbitrary"),
        name="proj_res",
    )(a, w, b.reshape(1, n), res)


def _pw1_glu_kernel(x_ref, nw_ref, wv_ref, wg_ref, bv_ref, bg_ref, o_ref, hn_ref):
    @pl.when(pl.program_id(1) == 0)
    def _():
        hn_ref[...] = _rms(x_ref[...], nw_ref[...]).astype(BF16)

    hn = hn_ref[...]
    val = _dot(hn, wv_ref[...]) + bv_ref[...]
    gate = _dot(hn, wg_ref[...]) + bg_ref[...]
    o_ref[...] = val * _sigmoid(gate)


def _pw1_glu(h, nw, w, b):
    t, d = h.shape
    tm = _tile(t, 1280, LANES)
    tn = _tile(d, 512, LANES)
    nj = d // tn
    b2 = b.reshape(1, 2 * d)
    return pl.pallas_call(
        _pw1_glu_kernel,
        grid=(t // tm, nj),
        in_specs=[
            pl.BlockSpec((tm, d), lambda i, j: (i, 0)),
            pl.BlockSpec((1, d), lambda i, j: (0, 0)),
            pl.BlockSpec((d, tn), lambda i, j: (0, j)),
            pl.BlockSpec((d, tn), lambda i, j: (0, j + nj)),
            pl.BlockSpec((1, tn), lambda i, j: (0, j)),
            pl.BlockSpec((1, tn), lambda i, j: (0, j + nj)),
        ],
        out_specs=pl.BlockSpec((tm, tn), lambda i, j: (i, j)),
        out_shape=jax.ShapeDtypeStruct((t, d), F32),
        scratch_shapes=[pltpu.VMEM((tm, d), BF16)],
        compiler_params=_params("parallel", "arbitrary"),
        name="conv_pw1_glu",
    )(h, nw.reshape(1, d), w, w, b2, b2)


def _dwconv_pw2_kernel(uc_ref, up_ref, wdw_ref, bdw_ref, lng_ref, lnb_ref, w2_ref, b2_ref,
                       r_ref, o_ref, xx_ref, y_ref, a_ref, *, tm, d):
    li = pl.program_id(1)
    rb = 64
    ncb = d // LANES

    @pl.when(pl.program_id(2) == 0)
    def _():
        pos = li * tm + lax.broadcasted_iota(jnp.int32, (tm, 1), 0)
        ppos = li * tm - CONV_HALO + lax.broadcasted_iota(jnp.int32, (CONV_HALO, 1), 0)
        for cb in range(ncb):
            cs = slice(cb * LANES, (cb + 1) * LANES)
            xx_ref[cb, 0:CONV_HALO, :] = jnp.where(ppos >= PAD_LEN, up_ref[0, :, cs], 0.0)
            xx_ref[cb, CONV_HALO:, :] = jnp.where(pos >= PAD_LEN, uc_ref[0, :, cs], 0.0)

        base = CONV_HALO - CONV_KERNEL + 1

        def col_block(cb, carry):
            for r in range(tm // rb):
                acc = jnp.zeros((rb, LANES), F32)
                for k in range(CONV_KERNEL):
                    r0 = r * rb + base + k
                    acc = acc + wdw_ref[cb, k:k + 1, :] * xx_ref[cb, r0:r0 + rb, :]
                y_ref[cb, r * rb:(r + 1) * rb, :] = acc + bdw_ref[cb]
            return carry

        lax.fori_loop(0, ncb, col_block, 0)

        total = y_ref[0]
        for cb in range(1, ncb):
            total = total + y_ref[cb]
        mu = jnp.sum(total, axis=-1, keepdims=True) * (1.0 / d)
        sq = jnp.zeros((tm, LANES), F32)
        for cb in range(ncb):
            yc = y_ref[cb] - mu
            sq = sq + yc * yc
        rstd = lax.rsqrt(jnp.sum(sq, axis=-1, keepdims=True) * (1.0 / d) + LN_EPS)
        for cb in range(ncb):
            ln = (y_ref[cb] - mu) * rstd * lng_ref[cb] + lnb_ref[cb]
            a_ref[:, cb * LANES:(cb + 1) * LANES] = _silu(ln).astype(BF16)

    o_ref[0] = r_ref[0] + _dot(a_ref[...], w2_ref[...]) + b2_ref[...]


def _dwconv_pw2(u, h, wdw, bdw, lng, lnb, w2, b2):
    bsz, length, d = u.shape
    tm = _tile(length, 640, 64)
    assert tm % CONV_HALO == 0
    tn = _tile(d, 512, LANES)
    halo_blocks = tm // CONV_HALO
    ncb = d // LANES
    wdw_p = jnp.zeros((CONV_HALO, d), F32).at[:CONV_KERNEL].set(wdw)
    wdw_p = wdw_p.reshape(CONV_HALO, ncb, LANES).transpose(1, 0, 2)
    row = lambda v: v.reshape(ncb, 1, LANES)
    whole = lambda shape: pl.BlockSpec(shape, lambda b, l, j: (0,) * len(shape))
    kern = functools.partial(_dwconv_pw2_kernel, tm=tm, d=d)
    return pl.pallas_call(
        kern,
        grid=(bsz, length // tm, d // tn),
        in_specs=[
            pl.BlockSpec((1, tm, d), lambda b, l, j: (b, l, 0)),
            pl.BlockSpec((1, CONV_HALO, d), lambda b, l, j: (b, jnp.maximum(l * halo_blocks - 1, 0), 0)),
            whole((ncb, CONV_HALO, LANES)),
            whole((ncb, 1, LANES)),
            whole((ncb, 1, LANES)),
            whole((ncb, 1, LANES)),
            pl.BlockSpec((d, tn), lambda b, l, j: (0, j)),
            pl.BlockSpec((1, tn), lambda b, l, j: (0, j)),
            pl.BlockSpec((1, tm, tn), lambda b, l, j: (b, l, j)),
        ],
        out_specs=pl.BlockSpec((1, tm, tn), lambda b, l, j: (b, l, j)),
        out_shape=jax.ShapeDtypeStruct((bsz, length, d), F32),
        scratch_shapes=[
            pltpu.VMEM((ncb, CONV_HALO + tm, LANES), F32),
            pltpu.VMEM((ncb, tm, LANES), F32),
            pltpu.VMEM((tm, d), BF16),
        ],
        compiler_params=_params("parallel", "parallel", "arbitrary"),
        name="conv_dw_pw2",
    )(u, u, wdw_p, row(bdw), row(lng), row(lnb), w2, b2.reshape(1, d), h)


def _qkv_rope_kernel(x_ref, nw_ref, w_ref, b_ref, cos_ref, sin_ref, o_ref, hn_ref, *, n_q, kv_width):
    j = pl.program_id(1)

    @pl.when(j == 0)
    def _():
        hn_ref[...] = _rms(x_ref[...], nw_ref[...]).astype(BF16)

    acc = _dot(hn_ref[...], w_ref[...]) + b_ref[...]
    tn = acc.shape[1]
    lane = lax.broadcasted_iota(jnp.int32, acc.shape, 1)
    first = (lane & (HEAD_DIM - 1)) < (HEAD_DIM // 2)
    rot = jnp.where(first, pltpu.roll(acc, tn - HEAD_DIM // 2, axis=1),
                    pltpu.roll(acc, HEAD_DIM // 2, axis=1))
    r = acc * cos_ref[...] + rot * sin_ref[...]
    scale = jnp.where(j < n_q, HEAD_DIM ** -0.5, 1.0)
    n_roped = jnp.where(j < n_q, tn, kv_width)
    o_ref[...] = jnp.where(lane < n_roped, r * scale, acc).astype(BF16)


def _qkv_rope(h, nw, w, b, cos_t, sin_t, length, q_width, kv_width):
    t, d = h.shape
    n = w.shape[1]
    tn = 2 * kv_width
    assert tn % LANES == 0 and q_width % tn == 0 and n == q_width + tn and cos_t.shape[1] == tn
    tm = _tile(length, 640, LANES)
    nl = length // tm
    kern = functools.partial(_qkv_rope_kernel, n_q=q_width // tn, kv_width=kv_width)
    return pl.pallas_call(
        kern,
        grid=(t // tm, n // tn),
        in_specs=[
            pl.BlockSpec((tm, d), lambda i, j: (i, 0)),
            pl.BlockSpec((1, d), lambda i, j: (0, 0)),
            pl.BlockSpec((d, tn), lambda i, j: (0, j)),
            pl.BlockSpec((1, tn), lambda i, j: (0, j)),
            pl.BlockSpec((tm, tn), lambda i, j: (i % nl, 0)),
            pl.BlockSpec((tm, tn), lambda i, j: (i % nl, 0)),
        ],
        out_specs=pl.BlockSpec((tm, tn), lambda i, j: (i, j)),
        out_shape=jax.ShapeDtypeStruct((t, n), BF16),
        scratch_shapes=[pltpu.VMEM((tm, d), BF16)],
        compiler_params=_params("parallel", "arbitrary"),
        name="attn_qkv_rope",
    )(h, nw.reshape(1, d), w, b.reshape(1, n), cos_t, sin_t)


def _attn_kernel(sink_ref, q_ref, kc_ref, kp_ref, vc_ref, vp_ref, o_ref, *, n_heads, n_kv):
    n = pl.program_id(1)
    blk = ATTN_BLOCK
    group = n_heads // n_kv
    qi = lax.broadcasted_iota(jnp.int32, (blk, 2 * blk), 0)
    ks = lax.broadcasted_iota(jnp.int32, (blk, 2 * blk), 1)
    k_idx = (n - 1) * blk + ks
    allowed = (ks > qi) & (ks <= qi + blk) & (k_idx >= PAD_LEN)
    head_cols = lambda h: slice(h * HEAD_DIM, (h + 1) * HEAD_DIM)
    kb = [jnp.concatenate([kp_ref[0, :, head_cols(kv)], kc_ref[0, :, head_cols(kv)]], axis=0)
          for kv in range(n_kv)]
    vb = [jnp.concatenate([vp_ref[0, :, head_cols(kv)], vc_ref[0, :, head_cols(kv)]], axis=0)
          for kv in range(n_kv)]
    for h0 in range(0, n_heads, group):
        heads = range(h0, h0 + group)
        sc = [jnp.where(allowed, _dot_nt(q_ref[0, :, head_cols(h)], kb[h // group]), NEG_INF) for h in heads]
        probs = []
        for h, s in zip(heads, sc):
            sink = sink_ref[h]
            m = jnp.maximum(jnp.max(s, axis=-1, keepdims=True), sink)
            e = jnp.exp(s - m)
            denom = jnp.sum(e, axis=-1, keepdims=True) + jnp.exp(sink - m)
            probs.append((e / denom).astype(BF16))
        outs = [_dot(p, vb[h // group]) for h, p in zip(heads, probs)]
        for h, o in zip(heads, outs):
            o_ref[0, :, head_cols(h)] = o.astype(BF16)


def _attention(qkv, sinks, n_heads, n_kv):
    bsz, length, _ = qkv.shape
    blk = ATTN_BLOCK
    qw = n_heads * HEAD_DIM
    kvw = n_kv * HEAD_DIM
    kcol = qw // kvw
    kern = functools.partial(_attn_kernel, n_heads=n_heads, n_kv=n_kv)
    prev = lambda n: jnp.maximum(n - 1, 0)
    return pl.pallas_call(
        kern,
        grid=(bsz, length // blk),
        in_specs=[
            pl.BlockSpec(memory_space=pltpu.SMEM),
            pl.BlockSpec((1, blk, qw), lambda b, n: (b, n, 0)),
            pl.BlockSpec((1, blk, kvw), lambda b, n: (b, n, kcol)),
            pl.BlockSpec((1, blk, kvw), lambda b, n: (b, prev(n), kcol)),
            pl.BlockSpec((1, blk, kvw), lambda b, n: (b, n, kcol + 1)),
            pl.BlockSpec((1, blk, kvw), lambda b, n: (b, prev(n), kcol + 1)),
        ],
        out_specs=pl.BlockSpec((1, blk, qw), lambda b, n: (b, n, 0)),
        out_shape=jax.ShapeDtypeStruct((bsz, length, qw), BF16),
        compiler_params=_params("parallel", "arbitrary"),
        name="attn_core",
    )(sinks, qkv, qkv, qkv, qkv, qkv)


def _gdn_in_kernel(x_ref, nw_ref, w_ref, wg_ref, o_ref, g_ref, hn_ref):
    @pl.when(pl.program_id(1) == 0)
    def _():
        hn_ref[...] = _rms(x_ref[...], nw_ref[...]).astype(BF16)
        g_ref[...] = _dot_nt(wg_ref[...], hn_ref[...])

    acc = _dot(hn_ref[...], w_ref[...])
    for c in range(o_ref.shape[0]):
        o_ref[c] = acc[:, c * LANES:(c + 1) * LANES].astype(BF16)


def _gdn_in(h, nw, w_main, w_gate_t):
    t, d = h.shape
    n = w_main.shape[1]
    tm = _tile(t, 1280, LANES)
    tn = _tile(n, 1024, LANES)
    cpb = tn // LANES
    return pl.pallas_call(
        _gdn_in_kernel,
        grid=(t // tm, n // tn),
        in_specs=[
            pl.BlockSpec((tm, d), lambda i, j: (i, 0)),
            pl.BlockSpec((1, d), lambda i, j: (0, 0)),
            pl.BlockSpec((d, tn), lambda i, j: (0, j)),
            pl.BlockSpec((LANES, d), lambda i, j: (0, 0)),
        ],
        out_specs=[
            pl.BlockSpec((cpb, tm, LANES), lambda i, j: (j, i, 0)),
            pl.BlockSpec((LANES, tm), lambda i, j: (0, i)),
        ],
        out_shape=[
            jax.ShapeDtypeStruct((n // LANES, t, LANES), BF16),
            jax.ShapeDtypeStruct((LANES, t), F32),
        ],
        scratch_shapes=[pltpu.VMEM((tm, d), BF16)],
        compiler_params=_params("parallel", "arbitrary"),
        name="gdn_in_proj",
    )(h, nw.reshape(1, d), w_main, w_gate_t)


def _gdn_kernel(alog_ref, dtb_ref, q_ref, k_ref, v_ref, z_ref, gate_ref, cwq_ref, cwk_ref, cwv_ref,
                nw_ref, o_ref, s_ref, tail_ref):
    ti = pl.program_id(2)
    rows = GDN_ROWS
    ch = GDN_CHUNK
    dk = GDN_DIM
    n_grp = q_ref.shape[0]

    @pl.when(ti == 0)
    def _():
        s_ref[...] = jnp.zeros_like(s_ref)
        tail_ref[...] = jnp.zeros_like(tail_ref)

    pos_col = ti * rows + lax.broadcasted_iota(jnp.int32, (rows, 1), 0)
    valid_col = pos_col >= PAD_LEN

    tails = [tail_ref[s] for s in range(4 * n_grp)]
    new_tails = [None] * (4 * n_grp)

    def conv_silu(x_raw, slot, w_ref, wi):
        x = jnp.where(valid_col, x_raw.astype(F32), 0.0)
        xx = jnp.concatenate([tails[slot], x], axis=0)
        new_tails[slot] = x[rows - SUBLANES:, :]
        y = w_ref[GDN_CONV - 1, wi] * x
        for k in range(GDN_CONV - 1):
            shifted = pltpu.roll(xx, GDN_CONV - 1 - k, axis=0)
            y = y + w_ref[k, wi] * shifted[SUBLANES:, :]
        return _silu(y)

    def l2n(x):
        return x * lax.rsqrt(jnp.sum(x * x, axis=-1, keepdims=True) + 1e-6)

    qs, ks, vs = [], [], []
    for g in range(n_grp):
        qs.append(l2n(conv_silu(q_ref[g, 0], 4 * g, cwq_ref, g)) * (dk ** -0.5))
        ks.append(l2n(conv_silu(k_ref[g, 0], 4 * g + 1, cwk_ref, g)))
        vs.append(conv_silu(v_ref[2 * g, 0], 4 * g + 2, cwv_ref, 2 * g))
        vs.append(conv_silu(v_ref[2 * g + 1, 0], 4 * g + 3, cwv_ref, 2 * g + 1))
    for s in range(4 * n_grp):
        tail_ref[s] = new_tails[s]

    n_vh = 2 * n_grp
    b_log = gate_ref[0, 0, 0]
    a_log = gate_ref[0, 1, 0]
    pos_row = ti * rows + lax.broadcasted_iota(jnp.int32, (n_vh, rows), 1)
    valid_row = pos_row >= PAD_LEN
    xs = a_log + dtb_ref[0]
    softplus = jnp.maximum(xs, 0.0) + jnp.log1p(jnp.exp(-jnp.abs(xs)))
    g_row = jnp.where(valid_row, -jnp.exp(alog_ref[0]) * softplus, 0.0)
    beta_row = jnp.where(valid_row, _sigmoid(b_log), 0.0)

    zpad = jnp.zeros((LANES - n_vh, rows), F32)
    g_pad = jnp.concatenate([g_row, zpad], axis=0)
    beta_pad = jnp.concatenate([beta_row, zpad], axis=0)
    ri = lax.broadcasted_iota(jnp.int32, (rows, rows), 0)
    ci = lax.broadcasted_iota(jnp.int32, (rows, rows), 1)
    shift = int(math.log2(ch))
    same_chunk = jnp.right_shift(ri, shift) == jnp.right_shift(ci, shift)
    upper = jnp.where(same_chunk & (ri <= ci), 1.0, 0.0).astype(BF16)
    lower = jnp.where(same_chunk & (ci <= ri), 1.0, 0.0).astype(BF16)
    eye = jnp.where(ri == ci, 1.0, 0.0).astype(BF16)
    g3 = _split3(g_pad)
    b3 = _split3(beta_pad)
    gc_row = _dot(g3[0], upper) + _dot(g3[1], upper) + _dot(g3[2], upper)
    gc_col = _dot_nt(lower, g3[0]) + _dot_nt(lower, g3[1]) + _dot_nt(lower, g3[2])
    beta_col = _dot_nt(eye, b3[0]) + _dot_nt(eye, b3[1]) + _dot_nt(eye, b3[2])

    ii = lax.broadcasted_iota(jnp.int32, (ch, ch), 0)
    jj = lax.broadcasted_iota(jnp.int32, (ch, ch), 1)
    causal = ii >= jj
    strict = ii > jj
    ident = jnp.where(ii == jj, 1.0, 0.0)

    n_ch = rows // ch
    pairs = [(c, hv) for c in range(n_ch) for hv in range(n_vh)]
    span = lambda c: slice(c * ch, (c + 1) * ch)
    kk, qk = {}, {}
    for c in range(n_ch):
        for g in range(n_grp):
            kc_b = ks[g][span(c)].astype(BF16)
            kk[c, g] = _dot_nt(kc_b, kc_b)
            qk[c, g] = _dot_nt(qs[g][span(c)].astype(BF16), kc_b)
    gcol, glast, bcol, decay, x, p = {}, {}, {}, {}, {}, {}
    for c, hv in pairs:
        gcol[c, hv] = gc_col[span(c), hv:hv + 1]
        grow = gc_row[hv:hv + 1, span(c)]
        glast[c, hv] = gc_col[(c + 1) * ch - 1:(c + 1) * ch, hv:hv + 1]
        bcol[c, hv] = beta_col[span(c), hv:hv + 1]
        decay[c, hv] = jnp.where(causal, jnp.exp(gcol[c, hv] - grow), 0.0)
        lm = jnp.where(strict, kk[c, hv // 2] * bcol[c, hv] * decay[c, hv], 0.0)
        x[c, hv] = ident - lm
        lm_b = lm.astype(BF16)
        p[c, hv] = _dot(lm_b, lm_b)
    for r in range(5):
        for key in pairs:
            p_b = p[key].astype(BF16)
            x[key] = x[key] + _dot(x[key].astype(BF16), p_b)
            if r < 4:
                p[key] = _dot(p_b, p_b)
    u, w, egc = {}, {}, {}
    for c, hv in pairs:
        kc = ks[hv // 2][span(c)]
        egc[c, hv] = jnp.exp(gcol[c, hv])
        rhs = jnp.concatenate([vs[hv][span(c)] * bcol[c, hv], kc * bcol[c, hv] * egc[c, hv]], axis=1)
        sol = _dot(x[c, hv].astype(BF16), rhs.astype(BF16))
        u[c, hv] = sol[:, :dk]
        w[c, hv] = sol[:, dk:]
    states = [s_ref[hv] for hv in range(n_vh)]
    for c in range(n_ch):
        ws_qs, v_new_b, outs = {}, {}, {}
        for hv in range(n_vh):
            q_dec = qs[hv // 2][span(c)] * egc[c, hv]
            wq = jnp.concatenate([w[c, hv], q_dec], axis=0).astype(BF16)
            ws_qs[hv] = _dot(wq, states[hv].astype(BF16))
        for hv in range(n_vh):
            v_new_b[hv] = (u[c, hv] - ws_qs[hv][:ch]).astype(BF16)
            intra = jnp.where(causal, qk[c, hv // 2] * decay[c, hv], 0.0).astype(BF16)
            outs[hv] = ws_qs[hv][ch:] + _dot(intra, v_new_b[hv])
        for hv in range(n_vh):
            k_dec = (ks[hv // 2][span(c)] * jnp.exp(glast[c, hv] - gcol[c, hv])).astype(BF16)
            states[hv] = states[hv] * jnp.exp(glast[c, hv]) + _dot_tn(k_dec, v_new_b[hv])
        for hv in range(n_vh):
            zc = z_ref[hv, 0, span(c), :].astype(F32)
            o_ref[hv, 0, span(c), :] = (_rms(outs[hv], nw_ref[...]) * _silu(zc)).astype(BF16)
    for hv in range(n_vh):
        s_ref[hv] = states[hv]


def _gdn_core(proj, gates, conv_w, a_log, dt_bias, norm_w, n_kh, grp):
    _, bsz, length, dk = proj.shape
    rows = GDN_ROWS
    ng = n_kh // grp
    return pl.pallas_call(
        _gdn_kernel,
        grid=(bsz, ng, length // rows),
        in_specs=[
            pl.BlockSpec((1, 2 * grp, rows), lambda b, h, t: (h, 0, 0)),
            pl.BlockSpec((1, 2 * grp, rows), lambda b, h, t: (h, 0, 0)),
            pl.BlockSpec((grp, 1, rows, dk), lambda b, h, t: (h, b, t, 0)),
            pl.BlockSpec((grp, 1, rows, dk), lambda b, h, t: (ng + h, b, t, 0)),
            pl.BlockSpec((2 * grp, 1, rows, dk), lambda b, h, t: (ng + h, b, t, 0)),
            pl.BlockSpec((2 * grp, 1, rows, dk), lambda b, h, t: (2 * ng + h, b, t, 0)),
            pl.BlockSpec((1, 2, 1, 2 * grp, rows), lambda b, h, t: (b, 0, h, 0, t)),
            pl.BlockSpec((GDN_CONV, grp, 1, dk), lambda b, h, t: (0, h, 0, 0)),
            pl.BlockSpec((GDN_CONV, grp, 1, dk), lambda b, h, t: (0, ng + h, 0, 0)),
            pl.BlockSpec((GDN_CONV, 2 * grp, 1, dk), lambda b, h, t: (0, ng + h, 0, 0)),
            pl.BlockSpec((1, dk), lambda b, h, t: (0, 0)),
        ],
        out_specs=pl.BlockSpec((2 * grp, 1, rows, dk), lambda b, h, t: (h, b, t, 0)),
        out_shape=jax.ShapeDtypeStruct((2 * n_kh, bsz, length, dk), BF16),
        scratch_shapes=[pltpu.VMEM((2 * grp, dk, dk), F32), pltpu.VMEM((4 * grp, SUBLANES, dk), F32)],
        compiler_params=_params("parallel", "parallel", "arbitrary"),
        name="gdn_core",
    )(a_log, dt_bias, proj, proj, proj, proj, gates, conv_w, conv_w, conv_w, norm_w.reshape(1, dk))


def _gdn_out_kernel(a_ref, w_ref, r_ref, o_ref, cat_ref):
    @pl.when(pl.program_id(1) == 0)
    def _():
        for hv in range(a_ref.shape[0]):
            cat_ref[:, hv * LANES:(hv + 1) * LANES] = a_ref[hv]

    o_ref[...] = r_ref[...] + _dot(cat_ref[...], w_ref[...])


def _gdn_out(o_heads, w, res):
    nh, t, dk = o_heads.shape
    n = w.shape[1]
    tm = _tile(t, 640, LANES)
    tn = _tile(n, 1024, LANES)
    return pl.pallas_call(
        _gdn_out_kernel,
        grid=(t // tm, n // tn),
        in_specs=[
            pl.BlockSpec((nh, tm, dk), lambda i, j: (0, i, 0)),
            pl.BlockSpec((nh * dk, tn), lambda i, j: (0, j)),
            pl.BlockSpec((tm, tn), lambda i, j: (i, j)),
        ],
        out_specs=pl.BlockSpec((tm, tn), lambda i, j: (i, j)),
        out_shape=jax.ShapeDtypeStruct((t, n), F32),
        scratch_shapes=[pltpu.VMEM((tm, nh * dk), BF16)],
        compiler_params=_params("parallel", "arbitrary"),
        name="gdn_out_proj",
    )(o_heads, w, res)


def _final_norm_kernel(x_ref, w_ref, o_ref):
    o_ref[0] = _rms(x_ref[0], w_ref[...])


def _final_norm(h, w, front):
    bsz, length, d = h.shape
    tm = front
    nfront = front // tm
    return pl.pallas_call(
        _final_norm_kernel,
        grid=(bsz, (length - front) // tm),
        in_specs=[
            pl.BlockSpec((1, tm, d), lambda b, i: (b, i + nfront, 0)),
            pl.BlockSpec((1, d), lambda b, i: (0, 0)),
        ],
        out_specs=pl.BlockSpec((1, tm, d), lambda b, i: (b, i, 0)),
        out_shape=jax.ShapeDtypeStruct((bsz, length - front, d), F32),
        compiler_params=_params("parallel", "parallel"),
        name="final_norm",
    )(h, w.reshape(1, d))


def _conformer_layer(h, nw, w_pw1, b_pw1, w_dw, b_dw, ln_g, ln_b, w_pw2, b_pw2):
    bsz, length, d = h.shape
    u = _pw1_glu(h.reshape(bsz * length, d), nw, w_pw1.astype(BF16), b_pw1)
    return _dwconv_pw2(u.reshape(bsz, length, d), h, w_dw, b_dw, ln_g, ln_b, w_pw2.astype(BF16), b_pw2)


def _rope_tables(length, width):
    pos = (jnp.arange(length) - PAD_LEN).astype(F32)
    inv_freq = ROPE_THETA ** (-jnp.arange(0, HEAD_DIM, 2, dtype=F32) / HEAD_DIM)
    ang = pos[:, None] * inv_freq[None, :]
    cos, sin = jnp.cos(ang), jnp.sin(ang)
    reps = width // HEAD_DIM
    cos_t = jnp.tile(jnp.concatenate([cos, cos], axis=1), (1, reps))
    sin_t = jnp.tile(jnp.concatenate([-sin, sin], axis=1), (1, reps))
    return cos_t, sin_t


def _attention_layer(h, nw, w_qkv, b_qkv, sinks, w_o, b_o):
    bsz, length, d = h.shape
    n_heads = sinks.shape[0]
    qw = n_heads * HEAD_DIM
    kvw = (w_qkv.shape[1] - qw) // 2
    cos_t, sin_t = _rope_tables(length, 2 * kvw)
    hf = h.reshape(bsz * length, d)
    qkv = _qkv_rope(hf, nw, w_qkv.astype(BF16), b_qkv, cos_t, sin_t, length, qw, kvw)
    att = _attention(qkv.reshape(bsz, length, -1), sinks.astype(F32), n_heads, kvw // HEAD_DIM)
    out = _proj_res(att.reshape(bsz * length, qw), w_o.astype(BF16), b_o, hf)
    return out.reshape(bsz, length, d)


def _gdn_layer(h, nw, w_in, conv_w, a_log, dt_bias, norm_w, w_out):
    bsz, length, d = h.shape
    n_vh = a_log.shape[0]
    n_kh = n_vh // 2
    dk = GDN_DIM
    main = 2 * n_kh * dk + 2 * n_vh * dk
    hf = h.reshape(bsz * length, d)
    w_gate_t = jnp.zeros((LANES, d), F32).at[:2 * n_vh].set(w_in[:, main:].T).astype(BF16)
    proj, gate_t = _gdn_in(hf, nw, w_in[:, :main].astype(BF16), w_gate_t)
    proj = proj.reshape(main // dk, bsz, length, dk)
    grp = GDN_GROUP if n_kh % GDN_GROUP == 0 else 1
    ng = n_kh // grp
    gates = gate_t[:2 * n_vh].reshape(2, ng, 2 * grp, bsz, length).transpose(3, 0, 1, 2, 4)
    conv_w3 = conv_w.reshape(GDN_CONV, -1, 1, dk)
    per_head = lambda v: jnp.broadcast_to(v.astype(F32).reshape(ng, 2 * grp, 1), (ng, 2 * grp, GDN_ROWS))
    o_heads = _gdn_core(proj, gates, conv_w3, per_head(a_log), per_head(dt_bias), norm_w, n_kh, grp)
    out = _gdn_out(o_heads.reshape(n_vh, bsz * length, dk), w_out.astype(BF16), hf)
    return out.reshape(bsz, length, d)


def kernel(x, meta_tokens, norm_mix, norm_ffn, norm_final, conv_w_pw1, conv_b_pw1, conv_w_dw, conv_b_dw, conv_ln_g, conv_ln_b, conv_w_pw2, conv_b_pw2, attn_w_qkv, attn_b_qkv, attn_sinks, attn_w_o, attn_b_o, gdn_w_in, gdn_conv_w, gdn_a_log, gdn_dt_bias, gdn_norm_w, gdn_w_out, ffn_w_gate, ffn_w_up, ffn_w_down):
    bsz, seq, d = x.shape
    depth = norm_mix.shape[0]
    meta = jnp.broadcast_to(meta_tokens.astype(x.dtype)[None], (bsz, N_META, d))
    h = jnp.concatenate([jnp.zeros((bsz, PAD_LEN, d), x.dtype), meta, x], axis=1)
    length = h.shape[1]
    assert length % ATTN_BLOCK == 0 and length % GDN_ROWS == 0

    for i in range(depth):
        kind, j = i % 3, i // 3
        if kind == 0:
            h = _conformer_layer(h, norm_mix[i], conv_w_pw1[j], conv_b_pw1[j], conv_w_dw[j], conv_b_dw[j],
                                 conv_ln_g[j], conv_ln_b[j], conv_w_pw2[j], conv_b_pw2[j])
        elif kind == 1:
            h = _attention_layer(h, norm_mix[i], attn_w_qkv[j], attn_b_qkv[j], attn_sinks[j],
                                 attn_w_o[j], attn_b_o[j])
        else:
            h = _gdn_layer(h, norm_mix[i], gdn_w_in[j], gdn_conv_w[j], gdn_a_log[j], gdn_dt_bias[j],
                           gdn_norm_w[j], gdn_w_out[j])
        hf = _ffn(h.reshape(bsz * length, d), norm_ffn[i], ffn_w_gate[i].astype(BF16),
                  ffn_w_up[i].astype(BF16), ffn_w_down[i].astype(BF16))
        h = hf.reshape(bsz, length, d)

    return _final_norm(h, norm_final, FRONT)
```

```python
import functools
import math

import jax
import jax.numpy as jnp
from jax import lax
from jax.experimental import pallas as pl
from jax.experimental.pallas import tpu as pltpu

F32 = jnp.float32
BF16 = jnp.bfloat16

N_META = 16
FRONT = 128
PAD_LEN = FRONT - N_META
NORM_EPS = 1e-6
LN_EPS = 1e-5
NEG_INF = -1e30
CONV_KERNEL = 31
HEAD_DIM = 64
ATTN_BLOCK = 128
ROPE_THETA = 10000.0
GDN_DIM = 128
GDN_CONV = 4
GDN_CHUNK = 64

LANES = 128
SUBLANES = 8
VMEM_LIMIT_BYTES = 56 * 1024 * 1024

CONV_HALO = 32
GDN_ROWS = 2 * GDN_CHUNK
GDN_GROUP = 8


def _params(*sem):
    return pltpu.CompilerParams(dimension_semantics=sem, vmem_limit_bytes=VMEM_LIMIT_BYTES)


def _tile(n, target, mult):
    best = None
    for d in range(mult, min(n, target) + 1, mult):
        if n % d == 0:
            best = d
    assert best is not None, (n, target, mult)
    return best


def _rms(x, w):
    ms = jnp.mean(x * x, axis=-1, keepdims=True)
    return x * lax.rsqrt(ms + NORM_EPS) * w


def _sigmoid(x):
    return 1.0 / (1.0 + jnp.exp(-x))


def _silu(x):
    return x * _sigmoid(x)


def _dot(a, b):
    return jnp.dot(a, b, preferred_element_type=F32)


def _dot_nt(a, b):
    return lax.dot_general(a, b, (((1,), (1,)), ((), ())), preferred_element_type=F32)


def _dot_tn(a, b):
    return lax.dot_general(a, b, (((0,), (0,)), ((), ())), preferred_element_type=F32)


def _split3(x):
    hi = x.astype(BF16)
    r1 = x - hi.astype(F32)
    mid = r1.astype(BF16)
    lo = (r1 - mid.astype(F32)).astype(BF16)
    return hi, mid, lo


def _ffn_kernel(x_ref, nw_ref, wg_ref, wu_ref, wd_ref, fw_ref, o_ref, hn_ref, acc_ref, *, final):
    f = pl.program_id(1)

    @pl.when(f == 0)
    def _():
        hn_ref[...] = _rms(x_ref[...], nw_ref[...]).astype(BF16)
        acc_ref[...] = jnp.zeros_like(acc_ref)

    hn = hn_ref[...]
    g = _dot(hn, wg_ref[...])
    u = _dot(hn, wu_ref[...])
    a = (_silu(g) * u).astype(BF16)
    acc_ref[...] += _dot(a, wd_ref[...])

    @pl.when(f == pl.num_programs(1) - 1)
    def _():
        y = x_ref[...] + acc_ref[...]
        o_ref[...] = _rms(y, fw_ref[...]) if final else y


def _ffn(h, nw, wg, wu, wd, final_w=None):
    t, d = h.shape
    f = wg.shape[1]
    tm = _tile(t, 640, LANES)
    tf = _tile(f, 512, LANES)
    final = final_w is not None
    fw = final_w if final else nw
    return pl.pallas_call(
        functools.partial(_ffn_kernel, final=final),
        grid=(t // tm, f // tf),
        in_specs=[
            pl.BlockSpec((tm, d), lambda i, j: (i, 0)),
            pl.BlockSpec((1, d), lambda i, j: (0, 0)),
            pl.BlockSpec((d, tf), lambda i, j: (0, j)),
            pl.BlockSpec((d, tf), lambda i, j: (0, j)),
            pl.BlockSpec((tf, d), lambda i, j: (j, 0)),
            pl.BlockSpec((1, d), lambda i, j: (0, 0)),
        ],
        out_specs=pl.BlockSpec((tm, d), lambda i, j: (i, 0)),
        out_shape=jax.ShapeDtypeStruct((t, d), F32),
        scratch_shapes=[pltpu.VMEM((tm, d), BF16), pltpu.VMEM((tm, d), F32)],
        compiler_params=_params("parallel", "arbitrary"),
        name="ffn",
    )(h, nw.reshape(1, d), wg, wu, wd, fw.reshape(1, d))


def _proj_res_kernel(a_ref, w_ref, b_ref, r_ref, o_ref):
    o_ref[...] = r_ref[...] + _dot(a_ref[...], w_ref[...]) + b_ref[...]


def _proj_res(a, w, b, res):
    t, k = a.shape
    n = w.shape[1]
    tm = _tile(t, 1280, LANES)
    tn = _tile(n, 512, LANES)
    return pl.pallas_call(
        _proj_res_kernel,
        grid=(t // tm, n // tn),
        in_specs=[
            pl.BlockSpec((tm, k), lambda i, j: (i, 0)),
            pl.BlockSpec((k, tn), lambda i, j: (0, j)),
            pl.BlockSpec((1, tn), lambda i, j: (0, j)),
            pl.BlockSpec((tm, tn), lambda i, j: (i, j)),
        ],
        out_specs=pl.BlockSpec((tm, tn), lambda i, j: (i, j)),
        out_shape=jax.ShapeDtypeStruct((t, n), F32),
        compiler_params=_params("parallel", "arbitrary"),
        name="proj_res",
    )(a, w, b.reshape(1, n), res)


def _pw1_glu_kernel(x_ref, nw_ref, wv_ref, wg_ref, bv_ref, bg_ref, o_ref, hn_ref):
    @pl.when(pl.program_id(1) == 0)
    def _():
        hn_ref[...] = _rms(x_ref[...], nw_ref[...]).astype(BF16)

    hn = hn_ref[...]
    val = _dot(hn, wv_ref[...]) + bv_ref[...]
    gate = _dot(hn, wg_ref[...]) + bg_ref[...]
    o_ref[...] = val * _sigmoid(gate)


def _pw1_glu(h, nw, w, b):
    t, d = h.shape
    tm = _tile(t, 1280, LANES)
    tn = _tile(d, 512, LANES)
    nj = d // tn
    b2 = b.reshape(1, 2 * d)
    return pl.pallas_call(
        _pw1_glu_kernel,
        grid=(t // tm, nj),
        in_specs=[
            pl.BlockSpec((tm, d), lambda i, j: (i, 0)),
            pl.BlockSpec((1, d), lambda i, j: (0, 0)),
            pl.BlockSpec((d, tn), lambda i, j: (0, j)),
            pl.BlockSpec((d, tn), lambda i, j: (0, j + nj)),
            pl.BlockSpec((1, tn), lambda i, j: (0, j)),
            pl.BlockSpec((1, tn), lambda i, j: (0, j + nj)),
        ],
        out_specs=pl.BlockSpec((tm, tn), lambda i, j: (i, j)),
        out_shape=jax.ShapeDtypeStruct((t, d), F32),
        scratch_shapes=[pltpu.VMEM((tm, d), BF16)],
        compiler_params=_params("parallel", "arbitrary"),
        name="conv_pw1_glu",
    )(h, nw.reshape(1, d), w, w, b2, b2)


def _dwconv_pw2_kernel(uc_ref, up_ref, wdw_ref, bdw_ref, lng_ref, lnb_ref, w2_ref, b2_ref,
                       r_ref, o_ref, xx_ref, y_ref, a_ref, *, tm, d):
    li = pl.program_id(1)
    rb = 64
    ncb = d // LANES

    @pl.when(pl.program_id(2) == 0)
    def _():
        pos = li * tm + lax.broadcasted_iota(jnp.int32, (tm, 1), 0)
        ppos = li * tm - CONV_HALO + lax.broadcasted_iota(jnp.int32, (CONV_HALO, 1), 0)
        for cb in range(ncb):
            cs = slice(cb * LANES, (cb + 1) * LANES)
            xx_ref[cb, 0:CONV_HALO, :] = jnp.where(ppos >= PAD_LEN, up_ref[0, :, cs], 0.0)
            xx_ref[cb, CONV_HALO:, :] = jnp.where(pos >= PAD_LEN, uc_ref[0, :, cs], 0.0)

        base = CONV_HALO - CONV_KERNEL + 1

        def col_block(cb, carry):
            for r in range(tm // rb):
                acc = jnp.zeros((rb, LANES), F32)
                for k in range(CONV_KERNEL):
                    r0 = r * rb + base + k
                    acc = acc + wdw_ref[cb, k:k + 1, :] * xx_ref[cb, r0:r0 + rb, :]
                y_ref[cb, r * rb:(r + 1) * rb, :] = acc + bdw_ref[cb]
            return carry

        lax.fori_loop(0, ncb, col_block, 0)

        total = y_ref[0]
        for cb in range(1, ncb):
            total = total + y_ref[cb]
        mu = jnp.sum(total, axis=-1, keepdims=True) * (1.0 / d)
        sq = jnp.zeros((tm, LANES), F32)
        for cb in range(ncb):
            yc = y_ref[cb] - mu
            sq = sq + yc * yc
        rstd = lax.rsqrt(jnp.sum(sq, axis=-1, keepdims=True) * (1.0 / d) + LN_EPS)
        for cb in range(ncb):
            ln = (y_ref[cb] - mu) * rstd * lng_ref[cb] + lnb_ref[cb]
            a_ref[:, cb * LANES:(cb + 1) * LANES] = _silu(ln).astype(BF16)

    o_ref[0] = r_ref[0] + _dot(a_ref[...], w2_ref[...]) + b2_ref[...]


def _dwconv_pw2(u, h, wdw, bdw, lng, lnb, w2, b2):
    bsz, length, d = u.shape
    tm = _tile(length, 640, 64)
    assert tm % CONV_HALO == 0
    tn = _tile(d, 512, LANES)
    halo_blocks = tm // CONV_HALO
    ncb = d // LANES
    wdw_p = jnp.zeros((CONV_HALO, d), F32).at[:CONV_KERNEL].set(wdw)
    wdw_p = wdw_p.reshape(CONV_HALO, ncb, LANES).transpose(1, 0, 2)
    row = lambda v: v.reshape(ncb, 1, LANES)
    whole = lambda shape: pl.BlockSpec(shape, lambda b, l, j: (0,) * len(shape))
    kern = functools.partial(_dwconv_pw2_kernel, tm=tm, d=d)
    return pl.pallas_call(
        kern,
        grid=(bsz, length // tm, d // tn),
        in_specs=[
            pl.BlockSpec((1, tm, d), lambda b, l, j: (b, l, 0)),
            pl.BlockSpec((1, CONV_HALO, d), lambda b, l, j: (b, jnp.maximum(l * halo_blocks - 1, 0), 0)),
            whole((ncb, CONV_HALO, LANES)),
            whole((ncb, 1, LANES)),
            whole((ncb, 1, LANES)),
            whole((ncb, 1, LANES)),
            pl.BlockSpec((d, tn), lambda b, l, j: (0, j)),
            pl.BlockSpec((1, tn), lambda b, l, j: (0, j)),
            pl.BlockSpec((1, tm, tn), lambda b, l, j: (b, l, j)),
        ],
        out_specs=pl.BlockSpec((1, tm, tn), lambda b, l, j: (b, l, j)),
        out_shape=jax.ShapeDtypeStruct((bsz, length, d), F32),
        scratch_shapes=[
            pltpu.VMEM((ncb, CONV_HALO + tm, LANES), F32),
            pltpu.VMEM((ncb, tm, LANES), F32),
            pltpu.VMEM((tm, d), BF16),
        ],
        compiler_params=_params("parallel", "parallel", "arbitrary"),
        name="conv_dw_pw2",
    )(u, u, wdw_p, row(bdw), row(lng), row(lnb), w2, b2.reshape(1, d), h)


def _qkv_rope_kernel(x_ref, nw_ref, w_ref, b_ref, cos_ref, sin_ref, o_ref, hn_ref, *, n_q, kv_width):
    j = pl.program_id(1)

    @pl.when(j == 0)
    def _():
        hn_ref[...] = _rms(x_ref[...], nw_ref[...]).astype(BF16)

    acc = _dot(hn_ref[...], w_ref[...]) + b_ref[...]
    tn = acc.shape[1]
    lane = lax.broadcasted_iota(jnp.int32, acc.shape, 1)
    first = (lane & (HEAD_DIM - 1)) < (HEAD_DIM // 2)
    rot = jnp.where(first, pltpu.roll(acc, tn - HEAD_DIM // 2, axis=1),
                    pltpu.roll(acc, HEAD_DIM // 2, axis=1))
    r = acc * cos_ref[...] + rot * sin_ref[...]
    scale = jnp.where(j < n_q, HEAD_DIM ** -0.5, 1.0)
    n_roped = jnp.where(j < n_q, tn, kv_width)
    o_ref[...] = jnp.where(lane < n_roped, r * scale, acc).astype(BF16)


def _qkv_rope(h, nw, w, b, cos_t, sin_t, length, q_width, kv_width):
    t, d = h.shape
    n = w.shape[1]
    tn = 2 * kv_width
    assert tn % LANES == 0 and q_width % tn == 0 and n == q_width + tn and cos_t.shape[1] == tn
    tm = _tile(length, 640, LANES)
    nl = length // tm
    kern = functools.partial(_qkv_rope_kernel, n_q=q_width // tn, kv_width=kv_width)
    return pl.pallas_call(
        kern,
        grid=(t // tm, n // tn),
        in_specs=[
            pl.BlockSpec((tm, d), lambda i, j: (i, 0)),
            pl.BlockSpec((1, d), lambda i, j: (0, 0)),
            pl.BlockSpec((d, tn), lambda i, j: (0, j)),
            pl.BlockSpec((1, tn), lambda i, j: (0, j)),
            pl.BlockSpec((tm, tn), lambda i, j: (i % nl, 0)),
            pl.BlockSpec((tm, tn), lambda i, j: (i % nl, 0)),
        ],
        out_specs=pl.BlockSpec((tm, tn), lambda i, j: (i, j)),
        out_shape=jax.ShapeDtypeStruct((t, n), BF16),
        scratch_shapes=[pltpu.VMEM((tm, d), BF16)],
        compiler_params=_params("parallel", "arbitrary"),
        name="attn_qkv_rope",
    )(h, nw.reshape(1, d), w, b.reshape(1, n), cos_t, sin_t)


def _attn_kernel(sink_ref, q_ref, kc_ref, kp_ref, vc_ref, vp_ref, o_ref, *, n_heads, n_kv):
    n = pl.program_id(1)
    blk = ATTN_BLOCK
    group = n_heads // n_kv
    qi = lax.broadcasted_iota(jnp.int32, (blk, 2 * blk), 0)
    ks = lax.broadcasted_iota(jnp.int32, (blk, 2 * blk), 1)
    k_idx = (n - 1) * blk + ks
    allowed = (ks > qi) & (ks <= qi + blk) & (k_idx >= PAD_LEN)
    head_cols = lambda h: slice(h * HEAD_DIM, (h + 1) * HEAD_DIM)
    kb = [jnp.concatenate([kp_ref[0, :, head_cols(kv)], kc_ref[0, :, head_cols(kv)]], axis=0)
          for kv in range(n_kv)]
    vb = [jnp.concatenate([vp_ref[0, :, head_cols(kv)], vc_ref[0, :, head_cols(kv)]], axis=0)
          for kv in range(n_kv)]
    for h0 in range(0, n_heads, group):
        heads = range(h0, h0 + group)
        sc = [jnp.where(allowed, _dot_nt(q_ref[0, :, head_cols(h)], kb[h // group]), NEG_INF) for h in heads]
        probs = []
        for h, s in zip(heads, sc):
            sink = sink_ref[h]
            m = jnp.maximum(jnp.max(s, axis=-1, keepdims=True), sink)
            e = jnp.exp(s - m)
            denom = jnp.sum(e, axis=-1, keepdims=True) + jnp.exp(sink - m)
            probs.append((e / denom).astype(BF16))
        outs = [_dot(p, vb[h // group]) for h, p in zip(heads, probs)]
        for h, o in zip(heads, outs):
            o_ref[0, :, head_cols(h)] = o.astype(BF16)


def _attention(qkv, sinks, n_heads, n_kv):
    bsz, length, _ = qkv.shape
    blk = ATTN_BLOCK
    qw = n_heads * HEAD_DIM
    kvw = n_kv * HEAD_DIM
    kcol = qw // kvw
    kern = functools.partial(_attn_kernel, n_heads=n_heads, n_kv=n_kv)
    prev = lambda n: jnp.maximum(n - 1, 0)
    return pl.pallas_call(
        kern,
        grid=(bsz, length // blk),
        in_specs=[
            pl.BlockSpec(memory_space=pltpu.SMEM),
            pl.BlockSpec((1, blk, qw), lambda b, n: (b, n, 0)),
            pl.BlockSpec((1, blk, kvw), lambda b, n: (b, n, kcol)),
            pl.BlockSpec((1, blk, kvw), lambda b, n: (b, prev(n), kcol)),
            pl.BlockSpec((1, blk, kvw), lambda b, n: (b, n, kcol + 1)),
            pl.BlockSpec((1, blk, kvw), lambda b, n: (b, prev(n), kcol + 1)),
        ],
        out_specs=pl.BlockSpec((1, blk, qw), lambda b, n: (b, n, 0)),
        out_shape=jax.ShapeDtypeStruct((bsz, length, qw), BF16),
        compiler_params=_params("parallel", "arbitrary"),
        name="attn_core",
    )(sinks, qkv, qkv, qkv, qkv, qkv)


def _gdn_in_kernel(x_ref, nw_ref, w_ref, wg_ref, o_ref, g_ref, hn_ref):
    @pl.when(pl.program_id(1) == 0)
    def _():
        hn_ref[...] = _rms(x_ref[...], nw_ref[...]).astype(BF16)
        g_ref[...] = _dot_nt(wg_ref[...], hn_ref[...])

    acc = _dot(hn_ref[...], w_ref[...])
    for c in range(o_ref.shape[0]):
        o_ref[c] = acc[:, c * LANES:(c + 1) * LANES].astype(BF16)


def _gdn_in(h, nw, w_main, w_gate_t):
    t, d = h.shape
    n = w_main.shape[1]
    tm = _tile(t, 1280, LANES)
    tn = _tile(n, 1024, LANES)
    cpb = tn // LANES
    return pl.pallas_call(
        _gdn_in_kernel,
        grid=(t // tm, n // tn),
        in_specs=[
            pl.BlockSpec((tm, d), lambda i, j: (i, 0)),
            pl.BlockSpec((1, d), lambda i, j: (0, 0)),
            pl.BlockSpec((d, tn), lambda i, j: (0, j)),
            pl.BlockSpec((LANES, d), lambda i, j: (0, 0)),
        ],
        out_specs=[
            pl.BlockSpec((cpb, tm, LANES), lambda i, j: (j, i, 0)),
            pl.BlockSpec((LANES, tm), lambda i, j: (0, i)),
        ],
        out_shape=[
            jax.ShapeDtypeStruct((n // LANES, t, LANES), BF16),
            jax.ShapeDtypeStruct((LANES, t), F32),
        ],
        scratch_shapes=[pltpu.VMEM((tm, d), BF16)],
        compiler_params=_params("parallel", "arbitrary"),
        name="gdn_in_proj",
    )(h, nw.reshape(1, d), w_main, w_gate_t)


def _gdn_kernel(alog_ref, dtb_ref, q_ref, k_ref, v_ref, z_ref, gate_ref, cwq_ref, cwk_ref, cwv_ref,
                nw_ref, o_ref, s_ref, tail_ref):
    ti = pl.program_id(2)
    rows = GDN_ROWS
    ch = GDN_CHUNK
    dk = GDN_DIM
    n_grp = q_ref.shape[0]

    @pl.when(ti == 0)
    def _():
        s_ref[...] = jnp.zeros_like(s_ref)
        tail_ref[...] = jnp.zeros_like(tail_ref)

    pos_col = ti * rows + lax.broadcasted_iota(jnp.int32, (rows, 1), 0)
    valid_col = pos_col >= PAD_LEN

    tails = [tail_ref[s] for s in range(4 * n_grp)]
    new_tails = [None] * (4 * n_grp)

    def conv_silu(x_raw, slot, w_ref, wi):
        x = jnp.where(valid_col, x_raw.astype(F32), 0.0)
        xx = jnp.concatenate([tails[slot], x], axis=0)
        new_tails[slot] = x[rows - SUBLANES:, :]
        y = w_ref[GDN_CONV - 1, wi] * x
        for k in range(GDN_CONV - 1):
            shifted = pltpu.roll(xx, GDN_CONV - 1 - k, axis=0)
            y = y + w_ref[k, wi] * shifted[SUBLANES:, :]
        return _silu(y)

    def l2n(x):
        return x * lax.rsqrt(jnp.sum(x * x, axis=-1, keepdims=True) + 1e-6)

    qs, ks, vs = [], [], []
    for g in range(n_grp):
        qs.append(l2n(conv_silu(q_ref[g, 0], 4 * g, cwq_ref, g)) * (dk ** -0.5))
        ks.append(l2n(conv_silu(k_ref[g, 0], 4 * g + 1, cwk_ref, g)))
        vs.append(conv_silu(v_ref[2 * g, 0], 4 * g + 2, cwv_ref, 2 * g))
        vs.append(conv_silu(v_ref[2 * g + 1, 0], 4 * g + 3, cwv_ref, 2 * g + 1))
    for s in range(4 * n_grp):
        tail_ref[s] = new_tails[s]

    n_vh = 2 * n_grp
    b_log = gate_ref[0, 0, 0]
    a_log = gate_ref[0, 1, 0]
    pos_row = ti * rows + lax.broadcasted_iota(jnp.int32, (n_vh, rows), 1)
    valid_row = pos_row >= PAD_LEN
    xs = a_log + dtb_ref[0]
    softplus = jnp.maximum(xs, 0.0) + jnp.log1p(jnp.exp(-jnp.abs(xs)))
    g_row = jnp.where(valid_row, -jnp.exp(alog_ref[0]) * softplus, 0.0)
    beta_row = jnp.where(valid_row, _sigmoid(b_log), 0.0)

    zpad = jnp.zeros((LANES - n_vh, rows), F32)
    g_pad = jnp.concatenate([g_row, zpad], axis=0)
    beta_pad = jnp.concatenate([beta_row, zpad], axis=0)
    ri = lax.broadcasted_iota(jnp.int32, (rows, rows), 0)
    ci = lax.broadcasted_iota(jnp.int32, (rows, rows), 1)
    shift = int(math.log2(ch))
    same_chunk = jnp.right_shift(ri, shift) == jnp.right_shift(ci, shift)
    upper = jnp.where(same_chunk & (ri <= ci), 1.0, 0.0).astype(BF16)
    lower = jnp.where(same_chunk & (ci <= ri), 1.0, 0.0).astype(BF16)
    eye = jnp.where(ri == ci, 1.0, 0.0).astype(BF16)
    g3 = _split3(g_pad)
    b3 = _split3(beta_pad)
    gc_row = _dot(g3[0], upper) + _dot(g3[1], upper) + _dot(g3[2], upper)
    gc_col = _dot_nt(lower, g3[0]) + _dot_nt(lower, g3[1]) + _dot_nt(lower, g3[2])
    beta_col = _dot_nt(eye, b3[0]) + _dot_nt(eye, b3[1]) + _dot_nt(eye, b3[2])

    ii = lax.broadcasted_iota(jnp.int32, (ch, ch), 0)
    jj = lax.broadcasted_iota(jnp.int32, (ch, ch), 1)
    causal = ii >= jj
    strict = ii > jj
    ident = jnp.where(ii == jj, 1.0, 0.0)

    n_ch = rows // ch
    pairs = [(c, hv) for c in range(n_ch) for hv in range(n_vh)]
    span = lambda c: slice(c * ch, (c + 1) * ch)
    kk, qk = {}, {}
    for c in range(n_ch):
        for g in range(n_grp):
            kc_b = ks[g][span(c)].astype(BF16)
            kk[c, g] = _dot_nt(kc_b, kc_b)
            qk[c, g] = _dot_nt(qs[g][span(c)].astype(BF16), kc_b)
    gcol, glast, bcol, decay, x, p = {}, {}, {}, {}, {}, {}
    for c, hv in pairs:
        gcol[c, hv] = gc_col[span(c), hv:hv + 1]
        grow = gc_row[hv:hv + 1, span(c)]
        glast[c, hv] = gc_col[(c + 1) * ch - 1:(c + 1) * ch, hv:hv + 1]
        bcol[c, hv] = beta_col[span(c), hv:hv + 1]
        decay[c, hv] = jnp.where(causal, jnp.exp(gcol[c, hv] - grow), 0.0)
        lm = jnp.where(strict, kk[c, hv // 2] * bcol[c, hv] * decay[c, hv], 0.0)
        x[c, hv] = ident - lm
        lm_b = lm.astype(BF16)
        p[c, hv] = _dot(lm_b, lm_b)
    for r in range(5):
        for key in pairs:
            p_b = p[key].astype(BF16)
            x[key] = x[key] + _dot(x[key].astype(BF16), p_b)
            if r < 4:
                p[key] = _dot(p_b, p_b)
    u, w, egc = {}, {}, {}
    for c, hv in pairs:
        kc = ks[hv // 2][span(c)]
        egc[c, hv] = jnp.exp(gcol[c, hv])
        rhs = jnp.concatenate([vs[hv][span(c)] * bcol[c, hv], kc * bcol[c, hv] * egc[c, hv]], axis=1)
        sol = _dot(x[c, hv].astype(BF16), rhs.astype(BF16))
        u[c, hv] = sol[:, :dk]
        w[c, hv] = sol[:, dk:]
    states = [s_ref[hv] for hv in range(n_vh)]
    for c in range(n_ch):
        ws_qs, v_new_b, outs = {}, {}, {}
        for hv in range(n_vh):
            q_dec = qs[hv // 2][span(c)] * egc[c, hv]
            wq = jnp.concatenate([w[c, hv], q_dec], axis=0).astype(BF16)
            ws_qs[hv] = _dot(wq, states[hv].astype(BF16))
        for hv in range(n_vh):
            v_new_b[hv] = (u[c, hv] - ws_qs[hv][:ch]).astype(BF16)
            intra = jnp.where(causal, qk[c, hv // 2] * decay[c, hv], 0.0).astype(BF16)
            outs[hv] = ws_qs[hv][ch:] + _dot(intra, v_new_b[hv])
        for hv in range(n_vh):
            k_dec = (ks[hv // 2][span(c)] * jnp.exp(glast[c, hv] - gcol[c, hv])).astype(BF16)
            states[hv] = states[hv] * jnp.exp(glast[c, hv]) + _dot_tn(k_dec, v_new_b[hv])
        for hv in range(n_vh):
            zc = z_ref[hv, 0, span(c), :].astype(F32)
            o_ref[hv, 0, span(c), :] = (_rms(outs[hv], nw_ref[...]) * _silu(zc)).astype(BF16)
    for hv in range(n_vh):
        s_ref[hv] = states[hv]


def _gdn_core(proj, gates, conv_w, a_log, dt_bias, norm_w, n_kh, grp):
    _, bsz, length, dk = proj.shape
    rows = GDN_ROWS
    ng = n_kh // grp
    return pl.pallas_call(
        _gdn_kernel,
        grid=(bsz, ng, length // rows),
        in_specs=[
            pl.BlockSpec((1, 2 * grp, rows), lambda b, h, t: (h, 0, 0)),
            pl.BlockSpec((1, 2 * grp, rows), lambda b, h, t: (h, 0, 0)),
            pl.BlockSpec((grp, 1, rows, dk), lambda b, h, t: (h, b, t, 0)),
            pl.BlockSpec((grp, 1, rows, dk), lambda b, h, t: (ng + h, b, t, 0)),
            pl.BlockSpec((2 * grp, 1, rows, dk), lambda b, h, t: (ng + h, b, t, 0)),
            pl.BlockSpec((2 * grp, 1, rows, dk), lambda b, h, t: (2 * ng + h, b, t, 0)),
            pl.BlockSpec((1, 2, 1, 2 * grp, rows), lambda b, h, t: (b, 0, h, 0, t)),
            pl.BlockSpec((GDN_CONV, grp, 1, dk), lambda b, h, t: (0, h, 0, 0)),
            pl.BlockSpec((GDN_CONV, grp, 1, dk), lambda b, h, t: (0, ng + h, 0, 0)),
            pl.BlockSpec((GDN_CONV, 2 * grp, 1, dk), lambda b, h, t: (0, ng + h, 0, 0)),
            pl.BlockSpec((1, dk), lambda b, h, t: (0, 0)),
        ],
        out_specs=pl.BlockSpec((2 * grp, 1, rows, dk), lambda b, h, t: (h, b, t, 0)),
        out_shape=jax.ShapeDtypeStruct((2 * n_kh, bsz, length, dk), BF16),
        scratch_shapes=[pltpu.VMEM((2 * grp, dk, dk), F32), pltpu.VMEM((4 * grp, SUBLANES, dk), F32)],
        compiler_params=_params("parallel", "parallel", "arbitrary"),
        name="gdn_core",
    )(a_log, dt_bias, proj, proj, proj, proj, gates, conv_w, conv_w, conv_w, norm_w.reshape(1, dk))


def _gdn_out_kernel(a_ref, w_ref, r_ref, o_ref, cat_ref):
    @pl.when(pl.program_id(1) == 0)
    def _():
        for hv in range(a_ref.shape[0]):
            cat_ref[:, hv * LANES:(hv + 1) * LANES] = a_ref[hv]

    o_ref[...] = r_ref[...] + _dot(cat_ref[...], w_ref[...])


def _gdn_out(o_heads, w, res):
    nh, t, dk = o_heads.shape
    n = w.shape[1]
    tm = _tile(t, 640, LANES)
    tn = _tile(n, 1024, LANES)
    return pl.pallas_call(
        _gdn_out_kernel,
        grid=(t // tm, n // tn),
        in_specs=[
            pl.BlockSpec((nh, tm, dk), lambda i, j: (0, i, 0)),
            pl.BlockSpec((nh * dk, tn), lambda i, j: (0, j)),
            pl.BlockSpec((tm, tn), lambda i, j: (i, j)),
        ],
        out_specs=pl.BlockSpec((tm, tn), lambda i, j: (i, j)),
        out_shape=jax.ShapeDtypeStruct((t, n), F32),
        scratch_shapes=[pltpu.VMEM((tm, nh * dk), BF16)],
        compiler_params=_params("parallel", "arbitrary"),
        name="gdn_out_proj",
    )(o_heads, w, res)


def _conformer_layer(h, nw, w_pw1, b_pw1, w_dw, b_dw, ln_g, ln_b, w_pw2, b_pw2):
    bsz, length, d = h.shape
    u = _pw1_glu(h.reshape(bsz * length, d), nw, w_pw1.astype(BF16), b_pw1)
    return _dwconv_pw2(u.reshape(bsz, length, d), h, w_dw, b_dw, ln_g, ln_b, w_pw2.astype(BF16), b_pw2)


def _rope_tables(length, width):
    pos = (jnp.arange(length) - PAD_LEN).astype(F32)
    inv_freq = ROPE_THETA ** (-jnp.arange(0, HEAD_DIM, 2, dtype=F32) / HEAD_DIM)
    ang = pos[:, None] * inv_freq[None, :]
    cos, sin = jnp.cos(ang), jnp.sin(ang)
    reps = width // HEAD_DIM
    cos_t = jnp.tile(jnp.concatenate([cos, cos], axis=1), (1, reps))
    sin_t = jnp.tile(jnp.concatenate([-sin, sin], axis=1), (1, reps))
    return cos_t, sin_t


def _attention_layer(h, nw, w_qkv, b_qkv, sinks, w_o, b_o):
    bsz, length, d = h.shape
    n_heads = sinks.shape[0]
    qw = n_heads * HEAD_DIM
    kvw = (w_qkv.shape[1] - qw) // 2
    cos_t, sin_t = _rope_tables(length, 2 * kvw)
    hf = h.reshape(bsz * length, d)
    qkv = _qkv_rope(hf, nw, w_qkv.astype(BF16), b_qkv, cos_t, sin_t, length, qw, kvw)
    att = _attention(qkv.reshape(bsz, length, -1), sinks.astype(F32), n_heads, kvw // HEAD_DIM)
    out = _proj_res(att.reshape(bsz * length, qw), w_o.astype(BF16), b_o, hf)
    return out.reshape(bsz, length, d)


def _gdn_layer(h, nw, w_in, conv_w, a_log, dt_bias, norm_w, w_out):
    bsz, length, d = h.shape
    n_vh = a_log.shape[0]
    n_kh = n_vh // 2
    dk = GDN_DIM
    main = 2 * n_kh * dk + 2 * n_vh * dk
    hf = h.reshape(bsz * length, d)
    w_gate_t = jnp.zeros((LANES, d), F32).at[:2 * n_vh].set(w_in[:, main:].T).astype(BF16)
    proj, gate_t = _gdn_in(hf, nw, w_in[:, :main].astype(BF16), w_gate_t)
    proj = proj.reshape(main // dk, bsz, length, dk)
    grp = GDN_GROUP if n_kh % GDN_GROUP == 0 else 1
    ng = n_kh // grp
    gates = gate_t[:2 * n_vh].reshape(2, ng, 2 * grp, bsz, length).transpose(3, 0, 1, 2, 4)
    conv_w3 = conv_w.reshape(GDN_CONV, -1, 1, dk)
    per_head = lambda v: jnp.broadcast_to(v.astype(F32).reshape(ng, 2 * grp, 1), (ng, 2 * grp, GDN_ROWS))
    o_heads = _gdn_core(proj, gates, conv_w3, per_head(a_log), per_head(dt_bias), norm_w, n_kh, grp)
    out = _gdn_out(o_heads.reshape(n_vh, bsz * length, dk), w_out.astype(BF16), hf)
    return out.reshape(bsz, length, d)


def kernel(x, meta_tokens, norm_mix, norm_ffn, norm_final, conv_w_pw1, conv_b_pw1, conv_w_dw, conv_b_dw, conv_ln_g, conv_ln_b, conv_w_pw2, conv_b_pw2, attn_w_qkv, attn_b_qkv, attn_sinks, attn_w_o, attn_b_o, gdn_w_in, gdn_conv_w, gdn_a_log, gdn_dt_bias, gdn_norm_w, gdn_w_out, ffn_w_gate, ffn_w_up, ffn_w_down):
    bsz, seq, d = x.shape
    depth = norm_mix.shape[0]
    meta = jnp.broadcast_to(meta_tokens.astype(x.dtype)[None], (bsz, N_META, d))
    h = jnp.concatenate([jnp.zeros((bsz, PAD_LEN, d), x.dtype), meta, x], axis=1)
    length = h.shape[1]
    assert length % ATTN_BLOCK == 0 and length % GDN_ROWS == 0

    for i in range(depth):
        kind, j = i % 3, i // 3
        if kind == 0:
            h = _conformer_layer(h, norm_mix[i], conv_w_pw1[j], conv_b_pw1[j], conv_w_dw[j], conv_b_dw[j],
                                 conv_ln_g[j], conv_ln_b[j], conv_w_pw2[j], conv_b_pw2[j])
        elif kind == 1:
            h = _attention_layer(h, norm_mix[i], attn_w_qkv[j], attn_b_qkv[j], attn_sinks[j],
                                 attn_w_o[j], attn_b_o[j])
        else:
            h = _gdn_layer(h, norm_mix[i], gdn_w_in[j], gdn_conv_w[j], gdn_a_log[j], gdn_dt_bias[j],
                           gdn_norm_w[j], gdn_w_out[j])
        hf = _ffn(h.reshape(bsz * length, d), norm_ffn[i], ffn_w_gate[i].astype(BF16),
                  ffn_w_up[i].astype(BF16), ffn_w_down[i].astype(BF16),
                  final_w=norm_final if i == depth - 1 else None)
        h = hf.reshape(bsz, length, d)

    return h[:, FRONT:]
```
